```python
import math
import jax
import jax.numpy as jnp
from jax import lax
import numpy as np

D_MODEL = 2048
BATCH = 8
SEQ = 2048
DEPTH = 2

MLA_HEADS = 8
QK_NOPE = 128
QK_ROPE = 64
V_HEAD = 128
Q_LORA = 512
KV_LORA = 512
ROPE_THETA = 10000.0
Q_BLOCK = 128
GDN_HEADS = 8
GDN_DK = 128
GDN_DV = 128
CONV_WIDTH = 4
CHUNK = 64
GDN_QK = GDN_HEADS * GDN_DK
GDN_V = GDN_HEADS * GDN_DV
CONV_CH = 2 * GDN_QK + GDN_V
D_FF = ((8 * D_MODEL // 3 + 255) // 256) * 256
EPS = 1e-6
IN_SIZES = (Q_LORA, KV_LORA, QK_ROPE, GDN_QK, GDN_QK, GDN_V, GDN_V, GDN_HEADS, GDN_HEADS, 2 * D_MODEL)
IN_WIDTH = Q_LORA + KV_LORA + QK_ROPE + 2 * GDN_QK + 2 * GDN_V + 2 * GDN_HEADS + 2 * D_MODEL

kernel_name = 'hybrid_mla_gdn_adaln_block'


def _rmsnorm(x, w):
    xf = x.astype(jnp.float32)
    y = xf * lax.rsqrt(jnp.mean(xf * xf, axis=-1, keepdims=True) + EPS)
    return (y * w.astype(jnp.float32)).astype(x.dtype)


def _split_cols(p):
    outs, off = [], 0
    for n in IN_SIZES:
        outs.append(p[..., off:off + n])
        off += n
    return outs


def _rope_tables(positions):
    inv_freq = 1.0 / (ROPE_THETA ** (jnp.arange(0, QK_ROPE, 2, dtype=jnp.float32) / QK_ROPE))
    ang = positions.astype(jnp.float32)[..., None] * inv_freq
    return jnp.cos(ang), jnp.sin(ang)


def _rope(x, cos, sin):
    xf = x.astype(jnp.float32)
    x1, x2 = jnp.split(xf, 2, axis=-1)
    return jnp.concatenate([x1 * cos - x2 * sin, x2 * cos + x1 * sin], axis=-1).astype(x.dtype)


def _mla_branch(c_q, c_kv, k_pe, q_norm, kv_norm, w_uq, w_ukv, cos, sin):
    B, T, _ = c_q.shape
    q = (_rmsnorm(c_q, q_norm) @ w_uq).reshape(B, T, MLA_HEADS, QK_NOPE + QK_ROPE)
    q_nope, q_pe = q[..., :QK_NOPE], q[..., QK_NOPE:]
    q_pe = _rope(q_pe, cos[:, :, None, :], sin[:, :, None, :])
    kv = (_rmsnorm(c_kv, kv_norm) @ w_ukv).reshape(B, T, MLA_HEADS, QK_NOPE + V_HEAD)
    k_nope, v = kv[..., :QK_NOPE], kv[..., QK_NOPE:]
    k_pe = _rope(k_pe, cos, sin)
    scale = (QK_NOPE + QK_ROPE) ** -0.5
    outs = []
    for i in range(T // Q_BLOCK):
        q0, k_end = i * Q_BLOCK, (i + 1) * Q_BLOCK
        s = (jnp.einsum('bqhd,bkhd->bhqk', q_nope[:, q0:k_end], k_nope[:, :k_end])
             + jnp.einsum('bqhr,bkr->bhqk', q_pe[:, q0:k_end], k_pe[:, :k_end]))
        s = s.astype(jnp.float32) * scale
        mask = jnp.arange(k_end)[None, :] <= (q0 + jnp.arange(Q_BLOCK))[:, None]
        p = jax.nn.softmax(jnp.where(mask, s, -jnp.inf), axis=-1).astype(v.dtype)
        outs.append(jnp.einsum('bhqk,bkhd->bqhd', p, v[:, :k_end]))
    return jnp.concatenate(outs, axis=1).reshape(B, T, MLA_HEADS * V_HEAD)


def _causal_conv_silu(u, w):
    kern = w[:, None, :].astype(u.dtype)
    y = lax.conv_general_dilated(u, kern, window_strides=(1,), padding=[(CONV_WIDTH - 1, 0)],
                                 dimension_numbers=('NWC', 'WIO', 'NWC'),
                                 feature_group_count=u.shape[-1])
    return jax.nn.silu(y)


def _l2norm(x):
    xf = x.astype(jnp.float32)
    return xf * lax.rsqrt(jnp.sum(xf * xf, axis=-1, keepdims=True) + EPS)


def _gated_delta_chunked(q, k, v, beta, g):
    B, T, H, DK = q.shape
    DV = v.shape[-1]
    N = T // CHUNK
    to_chunks = lambda a: a.reshape(B, N, CHUNK, H, -1).transpose(0, 3, 1, 2, 4)
    q, k, v = to_chunks(q), to_chunks(k), to_chunks(v)
    beta = beta.reshape(B, N, CHUNK, H).transpose(0, 3, 1, 2)
    g = g.reshape(B, N, CHUNK, H).transpose(0, 3, 1, 2)
    G = jnp.cumsum(g, axis=-1)
    idx = jnp.arange(CHUNK)
    lower = idx[:, None] >= idx[None, :]
    strict = idx[:, None] > idx[None, :]
    diff = G[..., :, None] - G[..., None, :]
    decay = jnp.where(lower, jnp.exp(jnp.where(lower, diff, 0.0)), 0.0)
    kb = k * beta[..., None]
    Lmat = jnp.where(strict, jnp.einsum('bhncd,bhnsd->bhncs', kb, k) * decay, 0.0)
    A = Lmat + jnp.eye(CHUNK, dtype=jnp.float32)
    rhs = jnp.concatenate([v * beta[..., None], kb * jnp.exp(G)[..., None]], axis=-1)
    sol = lax.linalg.triangular_solve(A, rhs, left_side=True, lower=True, unit_diagonal=True)
    u, w = sol[..., :DV], sol[..., DV:]
    attn = jnp.where(lower, jnp.einsum('bhncd,bhnsd->bhncs', q, k) * decay, 0.0)

    def step(S, inp):
        q_c, k_c, u_c, w_c, G_c, a_c = inp
        v_new = u_c - jnp.einsum('bhcd,bhde->bhce', w_c, S)
        o = (jnp.einsum('bhcd,bhde->bhce', q_c * jnp.exp(G_c)[..., None], S)
             + jnp.einsum('bhcs,bhse->bhce', a_c, v_new))
        G_last = G_c[..., -1]
        k_dec = k_c * jnp.exp(G_last[..., None] - G_c)[..., None]
        S = S * jnp.exp(G_last)[..., None, None] + jnp.einsum('bhcd,bhce->bhde', k_dec, v_new)
        return S, o

    xs = tuple(jnp.moveaxis(a, 2, 0) for a in (q, k, u, w, G, attn))
    S0 = jnp.zeros((B, H, DK, DV), jnp.float32)
    _, o = lax.scan(step, S0, xs)
    return o.transpose(1, 0, 3, 2, 4).reshape(B, T, H, DV)


def _gdn_branch(qkv, z, b_logit, a_logit, conv_w, A_log, dt_bias, gdn_norm):
    B, T, _ = qkv.shape
    dtype = qkv.dtype
    qkv = _causal_conv_silu(qkv, conv_w)
    q = _l2norm(qkv[..., :GDN_QK].reshape(B, T, GDN_HEADS, GDN_DK)) * (GDN_DK ** -0.5)
    k = _l2norm(qkv[..., GDN_QK:2 * GDN_QK].reshape(B, T, GDN_HEADS, GDN_DK))
    v = qkv[..., 2 * GDN_QK:].reshape(B, T, GDN_HEADS, GDN_DV).astype(jnp.float32)
    beta = jax.nn.sigmoid(b_logit.astype(jnp.float32))
    g = -jnp.exp(A_log.astype(jnp.float32)) * jax.nn.softplus(a_logit.astype(jnp.float32) + dt_bias.astype(jnp.float32))
    o = _gated_delta_chunked(q, k, v, beta, g)
    o = o * lax.rsqrt(jnp.mean(o * o, axis=-1, keepdims=True) + EPS) * gdn_norm.astype(jnp.float32)
    o = o * jax.nn.silu(z.reshape(B, T, GDN_HEADS, GDN_DV).astype(jnp.float32))
    return o.reshape(B, T, GDN_V).astype(dtype)


def setup_inputs(seed: int = 0) -> dict:
    key = jax.random.key(seed)
    ks = jax.random.split(key, 24)
    L, D = DEPTH, D_MODEL
    f32 = jnp.float32

    def nrm(k, shape, fan_in, gain=1.0):
        return gain * fan_in ** -0.5 * jax.random.normal(k, shape, f32)

    def gain(k, shape):
        return 1.0 + 0.02 * jax.random.normal(k, shape, f32)

    x = jax.random.normal(ks[0], (BATCH, SEQ, D), f32)
    c = jax.random.normal(ks[1], (BATCH, D), f32)
    positions = (jnp.arange(SEQ, dtype=jnp.int32)[None, :]
                 + jax.random.randint(ks[2], (BATCH, 1), 0, 1024, dtype=jnp.int32))
    dt = jnp.exp(jax.random.uniform(ks[15], (L, GDN_HEADS), f32, math.log(1e-3), math.log(1e-1)))
    return {
        'x': x,
        'c': c,
        'positions': positions,
        'w_ada': nrm(ks[3], (L, D, 6 * D), D, 0.5),
        'b_ada': 0.01 * jax.random.normal(ks[4], (L, 6 * D), f32),
        'norm_mix': gain(ks[5], (L, D)),
        'norm_ffn': gain(ks[6], (L, D)),
        'w_in': nrm(ks[7], (L, D, IN_WIDTH), D),
        'q_a_norm': gain(ks[8], (L, Q_LORA)),
        'kv_a_norm': gain(ks[9], (L, KV_LORA)),
        'w_uq': nrm(ks[10], (L, Q_LORA, MLA_HEADS * (QK_NOPE + QK_ROPE)), Q_LORA),
        'w_ukv': nrm(ks[11], (L, KV_LORA, MLA_HEADS * (QK_NOPE + V_HEAD)), KV_LORA),
        'w_o_mla': nrm(ks[12], (L, MLA_HEADS * V_HEAD, D), MLA_HEADS * V_HEAD),
        'conv_w': nrm(ks[13], (L, CONV_WIDTH, CONV_CH), CONV_WIDTH),
        'A_log': jnp.log(jax.random.uniform(ks[14], (L, GDN_HEADS), f32, 1.0, 16.0)),
        'dt_bias': dt + jnp.log(-jnp.expm1(-dt)),
        'gdn_norm': gain(ks[16], (L, GDN_DV)),
        'w_o_gdn': nrm(ks[17], (L, GDN_V, D), GDN_V),
        'w_o': nrm(ks[18], (L, D, D), D),
        'w_gate_up': nrm(ks[19], (L, D, 2 * D_FF), D),
        'w_down': nrm(ks[20], (L, D_FF, D), D_FF),
        'final_norm': gain(ks[21], (D,)),
    }


def reference(x, c, positions, w_ada, b_ada, norm_mix, norm_ffn, w_in, q_a_norm, kv_a_norm,
              w_uq, w_ukv, w_o_mla, conv_w, A_log, dt_bias, gdn_norm, w_o_gdn, w_o,
              w_gate_up, w_down, final_norm):
    cos, sin = _rope_tables(positions)
    c_act = jax.nn.silu(c)
    for l in range(DEPTH):
        mod = c_act @ w_ada[l] + b_ada[l]
        sh_a, sc_a, gt_a, sh_f, sc_f, gt_f = [m[:, None, :] for m in jnp.split(mod, 6, axis=-1)]
        h = _rmsnorm(x, norm_mix[l]) * (1.0 + sc_a) + sh_a
        p = h @ w_in[l]
        c_q, c_kv, k_pe, q_g, k_g, v_g, z, b_logit, a_logit, gate_logits = _split_cols(p)
        y_a = _mla_branch(c_q, c_kv, k_pe, q_a_norm[l], kv_a_norm[l], w_uq[l], w_ukv[l], cos, sin) @ w_o_mla[l]
        qkv = jnp.concatenate([q_g, k_g, v_g], axis=-1)
        y_b = _gdn_branch(qkv, z, b_logit, a_logit, conv_w[l], A_log[l], dt_bias[l], gdn_norm[l]) @ w_o_gdn[l]
        g_a, g_b = jnp.split(jax.nn.sigmoid(gate_logits), 2, axis=-1)
        mix = (g_a * y_a + g_b * y_b) @ w_o[l]
        x = x + gt_a * mix
        h = _rmsnorm(x, norm_ffn[l]) * (1.0 + sc_f) + sh_f
        gate, up = jnp.split(h @ w_gate_up[l], 2, axis=-1)
        x = x + gt_f * ((jax.nn.silu(gate) * up) @ w_down[l])
    return _rmsnorm(x, final_norm)
```

```python
import functools

import jax
import jax.numpy as jnp
from jax import lax
from jax.experimental import pallas as pl
from jax.experimental.pallas import tpu as pltpu

F32 = jnp.float32
BF16 = jnp.bfloat16

MLA_HEADS = 8
QK_NOPE = 128
QK_ROPE = 64
V_HEAD = 128
Q_LORA = 512
KV_LORA = 512
ROPE_THETA = 10000.0
GDN_HEADS = 8
GDN_DK = 128
GDN_DV = 128
CONV_WIDTH = 4
CHUNK = 64
EPS = 1e-6

LANES = 128
MLA_QK_PAD = 2 * LANES
VMEM_LIMIT_BYTES = 56 * 1024 * 1024

GDN_QK = GDN_HEADS * GDN_DK
GDN_V = GDN_HEADS * GDN_DV


def _params(*semantics):
    return pltpu.CompilerParams(dimension_semantics=semantics, vmem_limit_bytes=VMEM_LIMIT_BYTES)


def _sigmoid(x):
    return 1.0 / (1.0 + jnp.exp(-x))


def _silu(x):
    return x * _sigmoid(x)


def _rms(x, w):
    return x * lax.rsqrt(jnp.mean(x * x, axis=-1, keepdims=True) + EPS) * w


def _bdot(a, b):
    return jnp.dot(a.astype(BF16), b.astype(BF16), preferred_element_type=F32)


def _ada_body(c_ref, w_ref, b_ref, o_ref):
    c = c_ref[...]
    o_ref[...] = _bdot(_silu(c), w_ref[...]) + b_ref[...]


def _ada(c, w_ada, b_ada):
    depth, d, n6 = w_ada.shape
    b = c.shape[0]
    tn = 1024
    return pl.pallas_call(
        _ada_body,
        out_shape=jax.ShapeDtypeStruct((depth, b, n6), F32),
        grid=(depth, n6 // tn),
        in_specs=[pl.BlockSpec((b, d), lambda l, j: (0, 0)),
                  pl.BlockSpec((None, d, tn), lambda l, j: (l, 0, j)),
                  pl.BlockSpec((None, 1, tn), lambda l, j: (l, 0, j))],
        out_specs=pl.BlockSpec((None, b, tn), lambda l, j: (l, 0, j)),
        compiler_params=_params("arbitrary", "arbitrary"),
        name="ada_mod",
    )(c, w_ada, b_ada.reshape(depth, 1, n6))


def _inproj_body(x_ref, nw_ref, sc_ref, sh_ref, w_ref, ws_ref, o_ref, os_ref, h_scr):
    @pl.when(pl.program_id(1) == 0)
    def _():
        h = _rms(x_ref[...], nw_ref[...]) * (1.0 + sc_ref[...]) + sh_ref[...]
        hb = h.astype(BF16)
        h_scr[...] = hb
        os_ref[...] = jnp.dot(hb, ws_ref[...], preferred_element_type=F32)

    o_ref[...] = jnp.dot(h_scr[...], w_ref[...], preferred_element_type=F32).astype(o_ref.dtype)


def _inproj(x, nw, sc, sh, w_main, w_small, seq):
    n, d = x.shape
    nm = w_main.shape[1]
    ns = w_small.shape[1]
    tm, tn = min(1024, seq), 512
    per_b = seq // tm
    return pl.pallas_call(
        _inproj_body,
        out_shape=(jax.ShapeDtypeStruct((n, nm), BF16), jax.ShapeDtypeStruct((n, ns), F32)),
        grid=(n // tm, nm // tn),
        in_specs=[pl.BlockSpec((tm, d), lambda i, j: (i, 0)),
                  pl.BlockSpec((1, d), lambda i, j: (0, 0)),
                  pl.BlockSpec((None, 1, d), lambda i, j: (i // per_b, 0, 0)),
                  pl.BlockSpec((None, 1, d), lambda i, j: (i // per_b, 0, 0)),
                  pl.BlockSpec((d, tn), lambda i, j: (0, j)),
                  pl.BlockSpec((d, ns), lambda i, j: (0, 0))],
        out_specs=(pl.BlockSpec((tm, tn), lambda i, j: (i, j)),
                   pl.BlockSpec((tm, ns), lambda i, j: (i, 0))),
        scratch_shapes=[pltpu.VMEM((tm, d), BF16)],
        compiler_params=_params("arbitrary", "arbitrary"),
        name="in_proj",
    )(x, nw, sc, sh, w_main, w_small)


def _rope128(x, cos, sin):
    return x * cos + pltpu.roll(x, LANES // 2, 1) * sin


def _mla_proj_body(cq_ref, ckv_ref, kpe_ref, cos_ref, sin_ref, qn_ref, kvn_ref, wq_ref, wkv_ref,
                   q_ref, kv_ref, kpeo_ref):
    cos = cos_ref[...]
    sin = sin_ref[...]
    scale = (QK_NOPE + QK_ROPE) ** -0.5
    q = _bdot(_rms(cq_ref[...].astype(F32), qn_ref[...]), wq_ref[...])
    for h in range(MLA_HEADS):
        lo = h * MLA_QK_PAD
        q_ref[:, lo:lo + LANES] = (q[:, lo:lo + LANES] * scale).astype(BF16)
        pe = _rope128(q[:, lo + LANES:lo + 2 * LANES], cos, sin)
        q_ref[:, lo + LANES:lo + 2 * LANES] = (pe * scale).astype(BF16)
    kv_ref[...] = _bdot(_rms(ckv_ref[...].astype(F32), kvn_ref[...]), wkv_ref[...]).astype(BF16)
    kpeo_ref[...] = _rope128(kpe_ref[...], cos, sin).astype(BF16)


def _mla_proj(p_main, p_small, cos_t, sin_t, qn, kvn, wq, wkv, cq_blk, ckv_blk):
    n = p_main.shape[0]
    tm = 512
    nq, nkv = wq.shape[1], wkv.shape[1]
    return pl.pallas_call(
        _mla_proj_body,
        out_shape=(jax.ShapeDtypeStruct((n, nq), BF16), jax.ShapeDtypeStruct((n, nkv), BF16),
                   jax.ShapeDtypeStruct((n, LANES), BF16)),
        grid=(n // tm,),
        in_specs=[pl.BlockSpec((tm, Q_LORA), lambda i: (i, cq_blk)),
                  pl.BlockSpec((tm, KV_LORA), lambda i: (i, ckv_blk)),
                  pl.BlockSpec((tm, LANES), lambda i: (i, 0)),
                  pl.BlockSpec((tm, LANES), lambda i: (i, 0)),
                  pl.BlockSpec((tm, LANES), lambda i: (i, 0)),
                  pl.BlockSpec((1, Q_LORA), lambda i: (0, 0)),
                  pl.BlockSpec((1, KV_LORA), lambda i: (0, 0)),
                  pl.BlockSpec((Q_LORA, nq), lambda i: (0, 0)),
                  pl.BlockSpec((KV_LORA, nkv), lambda i: (0, 0))],
        out_specs=(pl.BlockSpec((tm, nq), lambda i: (i, 0)),
                   pl.BlockSpec((tm, nkv), lambda i: (i, 0)),
                   pl.BlockSpec((tm, LANES), lambda i: (i, 0))),
        compiler_params=_params("arbitrary"),
        name="mla_proj",
    )(p_main, p_main, p_small, cos_t, sin_t, qn, kvn, wq, wkv)


def _flash_body(q_ref, kv_ref, kpe_ref, o_ref, m_scr, l_scr, acc_scr, *, tq):
    i = pl.program_id(2)
    q = q_ref[...]
    m_scr[...] = jnp.full(m_scr.shape, -jnp.inf, F32)
    l_scr[...] = jnp.zeros(l_scr.shape, F32)
    acc_scr[...] = jnp.zeros(acc_scr.shape, F32)

    def step(j, masked):
        start = pl.multiple_of(j * tq, tq)
        k = jnp.concatenate([kv_ref[pl.ds(start, tq), 0:QK_NOPE], kpe_ref[pl.ds(start, tq), :]], axis=1)
        v = kv_ref[pl.ds(start, tq), QK_NOPE:QK_NOPE + V_HEAD]
        s = lax.dot_general(q, k, (((1,), (1,)), ((), ())), preferred_element_type=F32)
        if masked:
            row = lax.broadcasted_iota(jnp.int32, (tq, tq), 0)
            col = lax.broadcasted_iota(jnp.int32, (tq, tq), 1)
            s = jnp.where(row >= col, s, -jnp.inf)
        m_prev = m_scr[...]
        m_new = jnp.maximum(m_prev, jnp.max(s, axis=-1, keepdims=True))
        alpha = jnp.exp(m_prev - m_new)
        p = jnp.exp(s - m_new)
        l_scr[...] = alpha * l_scr[...] + jnp.sum(p, axis=-1, keepdims=True)
        acc_scr[...] = alpha * acc_scr[...] + jnp.dot(p.astype(BF16), v, preferred_element_type=F32)
        m_scr[...] = m_new

    def off_diag(j, carry):
        step(j, False)
        return carry

    lax.fori_loop(0, i, off_diag, 0)
    step(i, True)
    o_ref[...] = (acc_scr[...] / l_scr[...]).astype(o_ref.dtype)


def _flash(q, kv, kpe):
    b, t, _ = q.shape
    tq = min(512, t)
    return pl.pallas_call(
        functools.partial(_flash_body, tq=tq),
        out_shape=jax.ShapeDtypeStruct((b, t, MLA_HEADS * V_HEAD), BF16),
        grid=(b, MLA_HEADS, t // tq),
        in_specs=[pl.BlockSpec((None, tq, MLA_QK_PAD), lambda bi, h, i: (bi, i, h)),
                  pl.BlockSpec((None, t, QK_NOPE + V_HEAD), lambda bi, h, i: (bi, 0, h)),
                  pl.BlockSpec((None, t, LANES), lambda bi, h, i: (bi, 0, 0))],
        out_specs=pl.BlockSpec((None, tq, V_HEAD), lambda bi, h, i: (bi, i, h)),
        scratch_shapes=[pltpu.VMEM((tq, 1), F32), pltpu.VMEM((tq, 1), F32), pltpu.VMEM((tq, V_HEAD), F32)],
        compiler_params=_params("arbitrary", "arbitrary", "arbitrary"),
        name="mla_flash",
    )(q, kv, kpe)


def _conv_body(x_ref, w_ref, *out_refs, l2_scale, transpose_out):
    u = x_ref[...].astype(F32)
    w = w_ref[...]
    row = lax.broadcasted_iota(jnp.int32, u.shape, 0)
    y = u * w[CONV_WIDTH - 1:CONV_WIDTH, :]
    for s in range(1, CONV_WIDTH):
        shifted = jnp.where(row >= s, pltpu.roll(u, s, 0), 0.0)
        y = y + shifted * w[CONV_WIDTH - 1 - s:CONV_WIDTH - s, :]
    y = _silu(y)
    if l2_scale is not None:
        y = y * lax.rsqrt(jnp.sum(y * y, axis=-1, keepdims=True) + EPS) * l2_scale
    out_refs[0][...] = y.astype(BF16)
    if transpose_out:
        out_refs[1][...] = y.T.astype(BF16)


def _conv(p_main, conv_w, batch, seq, col_blk0, w_blk0, l2_scale, transpose_out):
    n = p_main.shape[0]
    heads = GDN_HEADS
    out_shape = [jax.ShapeDtypeStruct((n, heads * LANES), BF16)]
    out_specs = [pl.BlockSpec((seq, LANES), lambda b, h: (b, h))]
    if transpose_out:
        out_shape.append(jax.ShapeDtypeStruct((batch, heads, LANES, seq), BF16))
        out_specs.append(pl.BlockSpec((None, None, LANES, seq), lambda b, h: (b, h, 0, 0)))
    return pl.pallas_call(
        functools.partial(_conv_body, l2_scale=l2_scale, transpose_out=transpose_out),
        out_shape=tuple(out_shape),
        grid=(batch, heads),
        in_specs=[pl.BlockSpec((seq, LANES), lambda b, h: (b, col_blk0 + h)),
                  pl.BlockSpec((CONV_WIDTH, LANES), lambda b, h: (0, w_blk0 + h))],
        out_specs=tuple(out_specs),
        compiler_params=_params("arbitrary", "arbitrary"),
        name="gdn_conv",
    )(p_main, conv_w)


def _split3(x):
    hi = x.astype(BF16).astype(F32)
    mid = (x - hi).astype(BF16).astype(F32)
    lo = x - hi - mid
    return hi, mid, lo


def _gate_body(s_ref, alog_ref, dtb_ref, o_ref):
    tm = s_ref.shape[0]
    h = GDN_HEADS
    t = s_ref[...].T
    beta = _sigmoid(t[0:h])
    a = t[h:2 * h] + dtb_ref[...]
    softplus = jnp.maximum(a, 0.0) + jnp.log(1.0 + jnp.exp(-jnp.abs(a)))
    g = -jnp.exp(alog_ref[...]) * softplus
    r = lax.broadcasted_iota(jnp.int32, (tm, tm), 0)
    c = lax.broadcasted_iota(jnp.int32, (tm, tm), 1)
    shift = CHUNK.bit_length() - 1
    same = lax.shift_right_logical(r, shift) == lax.shift_right_logical(c, shift)
    cum_m = jnp.where(same & (r <= c), 1.0, 0.0).astype(BF16)
    tot_m = jnp.where(same, 1.0, 0.0).astype(BF16)
    parts = jnp.concatenate(_split3(g), axis=0).astype(BF16)
    cum3 = jnp.dot(parts, cum_m, preferred_element_type=F32)
    tot3 = jnp.dot(parts, tot_m, preferred_element_type=F32)
    gc = cum3[0:h] + cum3[h:2 * h] + cum3[2 * h:3 * h]
    gl = tot3[0:h] + tot3[h:2 * h] + tot3[2 * h:3 * h]
    o_ref[0:h, :] = beta
    o_ref[h:2 * h, :] = gc
    o_ref[2 * h:3 * h, :] = jnp.exp(gl - gc)
    o_ref[3 * h:4 * h, :] = jnp.exp(gl)


def _gates(p_small, a_log, dt_bias):
    n = p_small.shape[0]
    tm = 512
    h = GDN_HEADS
    return pl.pallas_call(
        _gate_body,
        out_shape=jax.ShapeDtypeStruct((4 * h, n), F32),
        grid=(n // tm,),
        in_specs=[pl.BlockSpec((tm, LANES), lambda i: (i, 1)),
                  pl.BlockSpec((h, 1), lambda i: (0, 0)),
                  pl.BlockSpec((h, 1), lambda i: (0, 0))],
        out_specs=pl.BlockSpec((4 * h, tm), lambda i: (0, i)),
        compiler_params=_params("arbitrary"),
        name="gdn_gates",
    )(p_small, a_log.reshape(h, 1), dt_bias.reshape(h, 1))


def _intra_body(q_ref, k_ref, v_ref, kt_ref, beta_ref, gc_ref, dec_ref,
                u_ref, w_ref, qe_ref, kdt_ref, attn_ref, *, nc):
    hh = pl.program_id(1)
    ri = lax.broadcasted_iota(jnp.int32, (CHUNK, CHUNK), 0)
    ci = lax.broadcasted_iota(jnp.int32, (CHUNK, CHUNK), 1)
    eye = ri == ci
    lower = ri >= ci
    strict = ri > ci
    kt = kt_ref[...]
    kdt_ref[...] = (kt.astype(F32) * dec_ref[pl.ds(hh, 1), :]).astype(BF16)

    def to_col(rowvec):
        return jnp.sum(jnp.where(eye, jnp.broadcast_to(rowvec, (CHUNK, CHUNK)), 0.0), axis=1, keepdims=True)

    for c in range(nc):
        rows = slice(c * CHUNK, (c + 1) * CHUNK)
        g_row = gc_ref[c:c + 1, :]
        g_col = to_col(g_row)
        b_col = to_col(beta_ref[c:c + 1, :])
        eg_col = jnp.exp(g_col)
        decay = jnp.where(lower, jnp.exp(jnp.where(lower, g_col - g_row, 0.0)), 0.0)
        kf = k_ref[rows, :].astype(F32)
        qb = q_ref[rows, :]
        vf = v_ref[rows, :].astype(F32)
        ktc = kt[:, c * CHUNK:(c + 1) * CHUNK]
        kb = kf * b_col
        kk = _bdot(kb, ktc)
        m1 = -jnp.where(strict, kk * decay, 0.0)
        m2 = _bdot(m1, m1)
        m4 = _bdot(m2, m2)
        m8 = _bdot(m4, m4)
        m16 = _bdot(m8, m8)
        m32 = _bdot(m16, m16)
        x1 = m1 + m2 + _bdot(m1, m2)
        x2 = m4 + m8 + _bdot(m4, m8)
        x3 = m16 + m32 + _bdot(m16, m32)
        x12 = x1 + x2 + _bdot(x1, x2)
        xt = x12 + x3 + _bdot(x12, x3)
        rhs = jnp.concatenate([vf * b_col, kb * eg_col], axis=1)
        sol = rhs + _bdot(xt, rhs)
        u_ref[rows, :] = sol[:, 0:GDN_DV].astype(BF16)
        w_ref[rows, :] = sol[:, GDN_DV:GDN_DV + GDN_DK].astype(BF16)
        qe_ref[rows, :] = (qb.astype(F32) * eg_col).astype(BF16)
        attn_ref[rows, :] = jnp.where(lower, _bdot(qb, ktc) * decay, 0.0).astype(BF16)


def _intra(qn, kn, vn, kt, gate_rows, gate_flat, batch, seq):
    n = qn.shape[0]
    h = GDN_HEADS
    tt = min(512, seq)
    nc = tt // CHUNK
    per_b = seq // tt
    wide = jax.ShapeDtypeStruct((n, h * LANES), BF16)
    row_spec = pl.BlockSpec((tt, LANES), lambda b, hh, t: (b * per_b + t, hh))
    return pl.pallas_call(
        functools.partial(_intra_body, nc=nc),
        out_shape=(wide, wide, wide,
                   jax.ShapeDtypeStruct((batch, h, LANES, seq), BF16),
                   jax.ShapeDtypeStruct((batch, h, seq, CHUNK), BF16)),
        grid=(batch, h, per_b),
        in_specs=[row_spec, row_spec, row_spec,
                  pl.BlockSpec((None, None, LANES, tt), lambda b, hh, t: (b, hh, 0, t)),
                  pl.BlockSpec((None, None, nc, CHUNK), lambda b, hh, t: (hh, b, t, 0)),
                  pl.BlockSpec((None, None, nc, CHUNK), lambda b, hh, t: (h + hh, b, t, 0)),
                  pl.BlockSpec((h, tt), lambda b, hh, t: (2, b * per_b + t))],
        out_specs=(row_spec, row_spec, row_spec,
                   pl.BlockSpec((None, None, LANES, tt), lambda b, hh, t: (b, hh, 0, t)),
                   pl.BlockSpec((None, None, tt, CHUNK), lambda b, hh, t: (b, hh, t, 0))),
        compiler_params=_params("arbitrary", "arbitrary", "arbitrary"),
        name="gdn_intra",
    )(qn, kn, vn, kt, gate_rows, gate_rows, gate_flat)


def _scan_body(u_ref, w_ref, qe_ref, kdt_ref, attn_ref, egl_ref, z_ref, gn_ref, o_ref, s_scr, *, nct):
    t = pl.program_id(1)

    @pl.when(t == 0)
    def _():
        s_scr[...] = jnp.zeros(s_scr.shape, F32)

    gn = gn_ref[...]
    for h in range(GDN_HEADS):
        cols = slice(h * LANES, (h + 1) * LANES)
        state = s_scr[h]
        for c in range(nct):
            rows = slice(c * CHUNK, (c + 1) * CHUNK)
            wq = jnp.concatenate([w_ref[rows, cols], qe_ref[rows, cols]], axis=0)
            r = jnp.dot(wq, state.astype(BF16), preferred_element_type=F32)
            v_new = (u_ref[rows, cols].astype(F32) - r[0:CHUNK]).astype(BF16)
            o = r[CHUNK:2 * CHUNK] + jnp.dot(attn_ref[h, rows, :], v_new, preferred_element_type=F32)
            eg = egl_ref[h, pl.ds(t * nct + c, 1), :][:, 0:1]
            state = state * eg + jnp.dot(kdt_ref[h, :, c * CHUNK:(c + 1) * CHUNK], v_new,
                                         preferred_element_type=F32)
            o = o * lax.rsqrt(jnp.mean(o * o, axis=-1, keepdims=True) + EPS) * gn
            o_ref[rows, cols] = (o * _silu(z_ref[rows, cols].astype(F32))).astype(BF16)
        s_scr[h] = state


def _scan(u, w, qe, kdt, attn, egl, p_main, gdn_norm, batch, seq, z_blk):
    n = u.shape[0]
    h = GDN_HEADS
    tt = min(256, seq)
    nct = tt // CHUNK
    per_b = seq // tt
    wide_spec = pl.BlockSpec((tt, h * LANES), lambda b, t: (b * per_b + t, 0))
    return pl.pallas_call(
        functools.partial(_scan_body, nct=nct),
        out_shape=jax.ShapeDtypeStruct((n, h * LANES), BF16),
        grid=(batch, per_b),
        in_specs=[wide_spec, wide_spec, wide_spec,
                  pl.BlockSpec((None, h, LANES, tt), lambda b, t: (b, 0, 0, t)),
                  pl.BlockSpec((None, h, tt, CHUNK), lambda b, t: (b, 0, t, 0)),
                  pl.BlockSpec((h, None, seq // CHUNK, CHUNK), lambda b, t: (3, b, 0, 0)),
                  pl.BlockSpec((tt, h * LANES), lambda b, t: (b * per_b + t, z_blk)),
                  pl.BlockSpec((1, LANES), lambda b, t: (0, 0))],
        out_specs=wide_spec,
        scratch_shapes=[pltpu.VMEM((h, GDN_DK, GDN_DV), F32)],
        compiler_params=_params("arbitrary", "arbitrary"),
        name="gdn_scan",
    )(u, w, qe, kdt, attn, egl, p_main, gdn_norm)


def _merge_body(a_ref, b_ref, wa_ref, wb_ref, ga_ref, gb_ref, o_ref):
    ya = jnp.dot(a_ref[...], wa_ref[...], preferred_element_type=F32)
    yb = jnp.dot(b_ref[...], wb_ref[...], preferred_element_type=F32)
    o = _sigmoid(ga_ref[...].astype(F32)) * ya + _sigmoid(gb_ref[...].astype(F32)) * yb
    o_ref[...] = o.astype(o_ref.dtype)


def _merge(o_a, o_b, w_a, w_b, p_main, d):
    n, ka = o_a.shape
    kb = o_b.shape[1]
    tm, tn = 1024, 512
    nb = d // tn
    return pl.pallas_call(
        _merge_body,
        out_shape=jax.ShapeDtypeStruct((n, d), BF16),
        grid=(n // tm, nb),
        in_specs=[pl.BlockSpec((tm, ka), lambda i, j: (i, 0)),
                  pl.BlockSpec((tm, kb), lambda i, j: (i, 0)),
                  pl.BlockSpec((ka, tn), lambda i, j: (0, j)),
                  pl.BlockSpec((kb, tn), lambda i, j: (0, j)),
                  pl.BlockSpec((tm, tn), lambda i, j: (i, j)),
                  pl.BlockSpec((tm, tn), lambda i, j: (i, nb + j))],
        out_specs=pl.BlockSpec((tm, tn), lambda i, j: (i, j)),
        compiler_params=_params("arbitrary", "arbitrary"),
        name="branch_merge",
    )(o_a, o_b, w_a, w_b, p_main, p_main)


def _resid_body(a_ref, w_ref, x_ref, gt_ref, o_ref):
    y = jnp.dot(a_ref[...], w_ref[...], preferred_element_type=F32)
    o_ref[...] = x_ref[...] + gt_ref[...] * y


def _resid_matmul(a, w, x, gt, seq, tm, tn):
    n, k = a.shape
    d = w.shape[1]
    tm = min(tm, seq)
    per_b = seq // tm
    return pl.pallas_call(
        _resid_body,
        out_shape=jax.ShapeDtypeStruct((n, d), F32),
        grid=(n // tm, d // tn),
        in_specs=[pl.BlockSpec((tm, k), lambda i, j: (i, 0)),
                  pl.BlockSpec((k, tn), lambda i, j: (0, j)),
                  pl.BlockSpec((tm, tn), lambda i, j: (i, j)),
                  pl.BlockSpec((None, 1, tn), lambda i, j: (i // per_b, 0, j))],
        out_specs=pl.BlockSpec((tm, tn), lambda i, j: (i, j)),
        compiler_params=_params("arbitrary", "arbitrary"),
        name="resid_matmul",
    )(a, w, x, gt)


def _ffn_up_body(x_ref, nw_ref, sc_ref, sh_ref, wg_ref, wu_ref, o_ref, h_scr):
    @pl.when(pl.program_id(1) == 0)
    def _():
        h = _rms(x_ref[...], nw_ref[...]) * (1.0 + sc_ref[...]) + sh_ref[...]
        h_scr[...] = h.astype(BF16)

    hb = h_scr[...]
    gate = jnp.dot(hb, wg_ref[...], preferred_element_type=F32)
    up = jnp.dot(hb, wu_ref[...], preferred_element_type=F32)
    o_ref[...] = (_silu(gate) * up).astype(o_ref.dtype)


def _ffn_up(x, nw, sc, sh, w_gu, seq):
    n, d = x.shape
    dff = w_gu.shape[1] // 2
    tm, tn = min(1024, seq), 512
    per_b = seq // tm
    nb = dff // tn
    return pl.pallas_call(
        _ffn_up_body,
        out_shape=jax.ShapeDtypeStruct((n, dff), BF16),
        grid=(n // tm, nb),
        in_specs=[pl.BlockSpec((tm, d), lambda i, j: (i, 0)),
                  pl.BlockSpec((1, d), lambda i, j: (0, 0)),
                  pl.BlockSpec((None, 1, d), lambda i, j: (i // per_b, 0, 0)),
                  pl.BlockSpec((None, 1, d), lambda i, j: (i // per_b, 0, 0)),
                  pl.BlockSpec((d, tn), lambda i, j: (0, j)),
                  pl.BlockSpec((d, tn), lambda i, j: (0, nb + j))],
        out_specs=pl.BlockSpec((tm, tn), lambda i, j: (i, j)),
        scratch_shapes=[pltpu.VMEM((tm, d), BF16)],
        compiler_params=_params("arbitrary", "arbitrary"),
        name="ffn_up",
    )(x, nw, sc, sh, w_gu, w_gu)


def _final_norm_body(x_ref, w_ref, o_ref):
    o_ref[...] = _rms(x_ref[...], w_ref[...])


def _final_norm(x, w):
    n, d = x.shape
    tm = 512
    return pl.pallas_call(
        _final_norm_body,
        out_shape=jax.ShapeDtypeStruct((n, d), F32),
        grid=(n // tm,),
        in_specs=[pl.BlockSpec((tm, d), lambda i: (i, 0)), pl.BlockSpec((1, d), lambda i: (0, 0))],
        out_specs=pl.BlockSpec((tm, d), lambda i: (i, 0)),
        compiler_params=_params("arbitrary"),
        name="final_norm",
    )(x, w)


def _pad_rope_cols(w):
    half = QK_ROPE // 2
    z = jnp.zeros(w.shape[:-1] + (LANES // 2 - half,), w.dtype)
    return jnp.concatenate([w[..., :half], z, w[..., half:], z], axis=-1)


def _layout_w_in(w_in, d):
    o = 0
    cq = w_in[:, o:o + Q_LORA]; o += Q_LORA
    ckv = w_in[:, o:o + KV_LORA]; o += KV_LORA
    kpe = w_in[:, o:o + QK_ROPE]; o += QK_ROPE
    qkvz = w_in[:, o:o + 2 * GDN_QK + 2 * GDN_V]; o += 2 * GDN_QK + 2 * GDN_V
    ba = w_in[:, o:o + 2 * GDN_HEADS]; o += 2 * GDN_HEADS
    gates = w_in[:, o:o + 2 * d]
    w_main = jnp.concatenate([gates, qkvz, cq, ckv], axis=1).astype(BF16)
    pad = jnp.zeros((w_in.shape[0], LANES - 2 * GDN_HEADS), w_in.dtype)
    w_small = jnp.concatenate([_pad_rope_cols(kpe), ba, pad], axis=1).astype(BF16)
    return w_main, w_small


def _layout_w_uq(w_uq):
    k = w_uq.shape[0]
    w = w_uq.reshape(k, MLA_HEADS, QK_NOPE + QK_ROPE)
    w = jnp.concatenate([w[..., :QK_NOPE], _pad_rope_cols(w[..., QK_NOPE:])], axis=-1)
    return w.reshape(k, MLA_HEADS * MLA_QK_PAD).astype(BF16)


def _rope_tables(positions):
    inv_freq = 1.0 / (ROPE_THETA ** (jnp.arange(0, QK_ROPE, 2, dtype=F32) / QK_ROPE))
    ang = positions.astype(F32).reshape(-1)[:, None] * inv_freq
    cos, sin = jnp.cos(ang), jnp.sin(ang)
    z = jnp.zeros_like(cos)
    return jnp.concatenate([cos, z, cos, z], axis=1), jnp.concatenate([-sin, z, sin, z], axis=1)


def kernel(x, c, positions, w_ada, b_ada, norm_mix, norm_ffn, w_in, q_a_norm, kv_a_norm, w_uq, w_ukv, w_o_mla,
           conv_w, A_log, dt_bias, gdn_norm, w_o_gdn, w_o, w_gate_up, w_down, final_norm):
    batch, seq, d = x.shape
    depth = w_ada.shape[0]
    n = batch * seq
    h = GDN_HEADS
    cos_t, sin_t = _rope_tables(positions)
    mod = _ada(c, w_ada, b_ada).reshape(depth, batch, 6, 1, d)
    xs = x.reshape(n, d)

    qkv_blk0 = 2 * d // LANES
    z_blk = (2 * d + 2 * GDN_QK + GDN_V) // (h * LANES)
    cq_blk = (2 * d + 2 * GDN_QK + 2 * GDN_V) // Q_LORA
    ckv_blk = cq_blk + 1

    for l in range(depth):
        sh_a, sc_a, gt_a, sh_f, sc_f, gt_f = [mod[l, :, i] for i in range(6)]
        w_main, w_small = _layout_w_in(w_in[l], d)
        p_main, p_small = _inproj(xs, norm_mix[l].reshape(1, d), sc_a, sh_a, w_main, w_small, seq)

        q, kv, kpe = _mla_proj(p_main, p_small, cos_t, sin_t, q_a_norm[l].reshape(1, -1),
                               kv_a_norm[l].reshape(1, -1), _layout_w_uq(w_uq[l]), w_ukv[l].astype(BF16),
                               cq_blk, ckv_blk)
        o_a = _flash(q.reshape(batch, seq, -1), kv.reshape(batch, seq, -1), kpe.reshape(batch, seq, -1))
        o_a = o_a.reshape(n, -1)

        (qn,) = _conv(p_main, conv_w[l], batch, seq, qkv_blk0, 0, GDN_DK ** -0.5, False)
        kn, kt = _conv(p_main, conv_w[l], batch, seq, qkv_blk0 + h, h, 1.0, True)
        (vn,) = _conv(p_main, conv_w[l], batch, seq, qkv_blk0 + 2 * h, 2 * h, None, False)
        gate_flat = _gates(p_small, A_log[l], dt_bias[l])
        gate_rows = gate_flat.reshape(4 * h, batch, seq // CHUNK, CHUNK)
        u, w, qe, kdt, attn = _intra(qn, kn, vn, kt, gate_rows, gate_flat, batch, seq)
        o_b = _scan(u, w, qe, kdt, attn, gate_rows, p_main, gdn_norm[l].reshape(1, -1), batch, seq, z_blk)

        merged = _merge(o_a, o_b, w_o_mla[l].astype(BF16), w_o_gdn[l].astype(BF16), p_main, d)
        xs = _resid_matmul(merged, w_o[l].astype(BF16), xs, gt_a, seq, 1024, 512)

        act = _ffn_up(xs, norm_ffn[l].reshape(1, d), sc_f, sh_f, w_gate_up[l].astype(BF16), seq)
        xs = _resid_matmul(act, w_down[l].astype(BF16), xs, gt_f, seq, 1024, 256)

    return _final_norm(xs, final_norm.reshape(1, d)).reshape(batch, seq, d)
```

```python
import functools
import math

import jax
import jax.numpy as jnp
from jax import lax
from jax.experimental import pallas as pl
from jax.experimental.pallas import tpu as pltpu

F32 = jnp.float32
BF16 = jnp.bfloat16

MLA_HEADS = 8
QK_NOPE = 128
QK_ROPE = 64
V_HEAD = 128
Q_LORA = 512
KV_LORA = 512
ROPE_THETA = 10000.0
GDN_HEADS = 8
GDN_DK = 128
GDN_DV = 128
CONV_WIDTH = 4
CHUNK = 64
EPS = 1e-6

LANES = 128
MLA_QK_PAD = 2 * LANES
VMEM_LIMIT_BYTES = 56 * 1024 * 1024

GDN_QK = GDN_HEADS * GDN_DK
GDN_V = GDN_HEADS * GDN_DV


def _params(*semantics):
    return pltpu.CompilerParams(dimension_semantics=semantics, vmem_limit_bytes=VMEM_LIMIT_BYTES)


def _sigmoid(x):
    return 1.0 / (1.0 + jnp.exp(-x))


def _silu(x):
    return x * _sigmoid(x)


def _rms(x, w):
    return x * lax.rsqrt(jnp.mean(x * x, axis=-1, keepdims=True) + EPS) * w


def _bdot(a, b):
    return jnp.dot(a.astype(BF16), b.astype(BF16), preferred_element_type=F32)


def _ada_body(c_ref, w_ref, b_ref, o_ref):
    c = c_ref[...]
    o_ref[...] = _bdot(_silu(c), w_ref[...]) + b_ref[...]


def _ada(c, w_ada, b_ada):
    depth, d, n6 = w_ada.shape
    b = c.shape[0]
    tn = 1024
    return pl.pallas_call(
        _ada_body,
        out_shape=jax.ShapeDtypeStruct((depth, b, n6), F32),
        grid=(depth, n6 // tn),
        in_specs=[pl.BlockSpec((b, d), lambda l, j: (0, 0)),
                  pl.BlockSpec((None, d, tn), lambda l, j: (l, 0, j)),
                  pl.BlockSpec((None, 1, tn), lambda l, j: (l, 0, j))],
        out_specs=pl.BlockSpec((None, b, tn), lambda l, j: (l, 0, j)),
        compiler_params=_params("arbitrary", "arbitrary"),
        name="ada_mod",
    )(c, w_ada, b_ada.reshape(depth, 1, n6))


def _inproj_body(x_ref, nw_ref, sc_ref, sh_ref, w_ref, ws_ref, o_ref, os_ref, h_scr):
    @pl.when(pl.program_id(1) == 0)
    def _():
        h = _rms(x_ref[...], nw_ref[...]) * (1.0 + sc_ref[...]) + sh_ref[...]
        hb = h.astype(BF16)
        h_scr[...] = hb
        os_ref[...] = jnp.dot(hb, ws_ref[...], preferred_element_type=F32)

    o_ref[...] = jnp.dot(h_scr[...], w_ref[...], preferred_element_type=F32).astype(o_ref.dtype)


def _inproj(x, nw, sc, sh, w_main, w_small, seq):
    n, d = x.shape
    nm = w_main.shape[1]
    ns = w_small.shape[1]
    tm, tn = min(1024, seq), 512
    per_b = seq // tm
    return pl.pallas_call(
        _inproj_body,
        out_shape=(jax.ShapeDtypeStruct((n, nm), BF16), jax.ShapeDtypeStruct((n, ns), F32)),
        grid=(n // tm, nm // tn),
        in_specs=[pl.BlockSpec((tm, d), lambda i, j: (i, 0)),
                  pl.BlockSpec((1, d), lambda i, j: (0, 0)),
                  pl.BlockSpec((None, 1, d), lambda i, j: (i // per_b, 0, 0)),
                  pl.BlockSpec((None, 1, d), lambda i, j: (i // per_b, 0, 0)),
                  pl.BlockSpec((d, tn), lambda i, j: (0, j)),
                  pl.BlockSpec((d, ns), lambda i, j: (0, 0))],
        out_specs=(pl.BlockSpec((tm, tn), lambda i, j: (i, j)),
                   pl.BlockSpec((tm, ns), lambda i, j: (i, 0))),
        scratch_shapes=[pltpu.VMEM((tm, d), BF16)],
        compiler_params=_params("arbitrary", "arbitrary"),
        name="in_proj",
    )(x, nw, sc, sh, w_main, w_small)


def _rope128(x, cos, sin):
    return x * cos + pltpu.roll(x, LANES // 2, 1) * sin


def _mla_proj_body(cq_ref, ckv_ref, kpe_ref, cos_ref, sin_ref, qn_ref, kvn_ref, wq_ref, wkv_ref,
                   q_ref, kv_ref, kpeo_ref):
    cos = cos_ref[...]
    sin = sin_ref[...]
    scale = (QK_NOPE + QK_ROPE) ** -0.5 * math.log2(math.e)
    q = _bdot(_rms(cq_ref[...].astype(F32), qn_ref[...]), wq_ref[...])
    for h in range(MLA_HEADS):
        lo = h * MLA_QK_PAD
        q_ref[:, lo:lo + LANES] = (q[:, lo:lo + LANES] * scale).astype(BF16)
        pe = _rope128(q[:, lo + LANES:lo + 2 * LANES], cos, sin)
        q_ref[:, lo + LANES:lo + 2 * LANES] = (pe * scale).astype(BF16)
    kv_ref[...] = _bdot(_rms(ckv_ref[...].astype(F32), kvn_ref[...]), wkv_ref[...]).astype(BF16)
    kpeo_ref[...] = _rope128(kpe_ref[...], cos, sin).astype(BF16)


def _mla_proj(p_main, p_small, cos_t, sin_t, qn, kvn, wq, wkv, cq_blk, ckv_blk):
    n = p_main.shape[0]
    tm = 512
    nq, nkv = wq.shape[1], wkv.shape[1]
    return pl.pallas_call(
        _mla_proj_body,
        out_shape=(jax.ShapeDtypeStruct((n, nq), BF16), jax.ShapeDtypeStruct((n, nkv), BF16),
                   jax.ShapeDtypeStruct((n, LANES), BF16)),
        grid=(n // tm,),
        in_specs=[pl.BlockSpec((tm, Q_LORA), lambda i: (i, cq_blk)),
                  pl.BlockSpec((tm, KV_LORA), lambda i: (i, ckv_blk)),
                  pl.BlockSpec((tm, LANES), lambda i: (i, 0)),
                  pl.BlockSpec((tm, LANES), lambda i: (i, 0)),
                  pl.BlockSpec((tm, LANES), lambda i: (i, 0)),
                  pl.BlockSpec((1, Q_LORA), lambda i: (0, 0)),
                  pl.BlockSpec((1, KV_LORA), lambda i: (0, 0)),
                  pl.BlockSpec((Q_LORA, nq), lambda i: (0, 0)),
                  pl.BlockSpec((KV_LORA, nkv), lambda i: (0, 0))],
        out_specs=(pl.BlockSpec((tm, nq), lambda i: (i, 0)),
                   pl.BlockSpec((tm, nkv), lambda i: (i, 0)),
                   pl.BlockSpec((tm, LANES), lambda i: (i, 0))),
        compiler_params=_params("arbitrary"),
        name="mla_proj",
    )(p_main, p_main, p_small, cos_t, sin_t, qn, kvn, wq, wkv)


def _flash_body(q_ref, kv_ref, kpe_ref, o_ref, *, tq, nq):
    tiles = [slice(i * tq, (i + 1) * tq) for i in range(nq)]
    row = lax.broadcasted_iota(jnp.int32, (tq, tq), 0)
    col = lax.broadcasted_iota(jnp.int32, (tq, tq), 1)
    causal = row >= col

    def scores(r):
        k = jnp.concatenate([kv_ref[tiles[r], 0:QK_NOPE], kpe_ref[tiles[r], :]], axis=1)
        return [lax.dot_general(q_ref[tiles[i], :], k, (((1,), (1,)), ((), ())), preferred_element_type=F32)
                for i in range(r, nq)]

    m = [None] * nq
    l = [None] * nq
    acc = [None] * nq
    s_next = scores(0)
    for r in range(nq):
        s_cur = s_next
        if r + 1 < nq:
            s_next = scores(r + 1)
        v = kv_ref[tiles[r], QK_NOPE:QK_NOPE + V_HEAD]
        p = []
        alpha = []
        for i, s in zip(range(r, nq), s_cur):
            if i == r:
                s = jnp.where(causal, s, -jnp.inf)
            s_max = jnp.max(s, axis=-1, keepdims=True)
            m_new = s_max if r == 0 else jnp.maximum(m[i], s_max)
            e = jnp.exp2(s - m_new)
            e_sum = jnp.sum(e, axis=-1, keepdims=True)
            if r == 0:
                alpha.append(None)
                l[i] = e_sum
            else:
                a = jnp.exp2(m[i] - m_new)
                alpha.append(a)
                l[i] = a * l[i] + e_sum
            m[i] = m_new
            p.append(e.astype(BF16))
        for i, pi, a in zip(range(r, nq), p, alpha):
            pv = jnp.dot(pi, v, preferred_element_type=F32)
            acc[i] = pv if a is None else a * acc[i] + pv
        o_ref[tiles[r], :] = (acc[r] / l[r]).astype(o_ref.dtype)


def _flash(q, kv, kpe):
    b, t, _ = q.shape
    tq = min(512, t)
    return pl.pallas_call(
        functools.partial(_flash_body, tq=tq, nq=t // tq),
        out_shape=jax.ShapeDtypeStruct((b, t, MLA_HEADS * V_HEAD), BF16),
        grid=(b, MLA_HEADS),
        in_specs=[pl.BlockSpec((None, t, MLA_QK_PAD), lambda bi, h: (bi, 0, h)),
                  pl.BlockSpec((None, t, QK_NOPE + V_HEAD), lambda bi, h: (bi, 0, h)),
                  pl.BlockSpec((None, t, LANES), lambda bi, h: (bi, 0, 0))],
        out_specs=pl.BlockSpec((None, t, V_HEAD), lambda bi, h: (bi, 0, h)),
        compiler_params=_params("arbitrary", "arbitrary"),
        name="mla_flash",
    )(q, kv, kpe)


def _conv_body(x_ref, w_ref, *out_refs, l2_scale, transpose_out):
    u = x_ref[...].astype(F32)
    w = w_ref[...]
    row = lax.broadcasted_iota(jnp.int32, u.shape, 0)
    y = u * w[CONV_WIDTH - 1:CONV_WIDTH, :]
    for s in range(1, CONV_WIDTH):
        shifted = jnp.where(row >= s, pltpu.roll(u, s, 0), 0.0)
        y = y + shifted * w[CONV_WIDTH - 1 - s:CONV_WIDTH - s, :]
    y = _silu(y)
    if l2_scale is not None:
        y = y * lax.rsqrt(jnp.sum(y * y, axis=-1, keepdims=True) + EPS) * l2_scale
    out_refs[0][...] = y.astype(BF16)
    if transpose_out:
        out_refs[1][...] = y.T.astype(BF16)


def _conv(p_main, conv_w, batch, seq, col_blk0, w_blk0, l2_scale, transpose_out):
    n = p_main.shape[0]
    heads = GDN_HEADS
    out_shape = [jax.ShapeDtypeStruct((n, heads * LANES), BF16)]
    out_specs = [pl.BlockSpec((seq, LANES), lambda b, h: (b, h))]
    if transpose_out:
        out_shape.append(jax.ShapeDtypeStruct((batch, heads, LANES, seq), BF16))
        out_specs.append(pl.BlockSpec((None, None, LANES, seq), lambda b, h: (b, h, 0, 0)))
    return pl.pallas_call(
        functools.partial(_conv_body, l2_scale=l2_scale, transpose_out=transpose_out),
        out_shape=tuple(out_shape),
        grid=(batch, heads),
        in_specs=[pl.BlockSpec((seq, LANES), lambda b, h: (b, col_blk0 + h)),
                  pl.BlockSpec((CONV_WIDTH, LANES), lambda b, h: (0, w_blk0 + h))],
        out_specs=tuple(out_specs),
        compiler_params=_params("arbitrary", "arbitrary"),
        name="gdn_conv",
    )(p_main, conv_w)


def _split3(x):
    hi = x.astype(BF16).astype(F32)
    mid = (x - hi).astype(BF16).astype(F32)
    lo = x - hi - mid
    return hi, mid, lo


def _gate_body(s_ref, alog_ref, dtb_ref, o_ref):
    tm = s_ref.shape[0]
    h = GDN_HEADS
    t = s_ref[...].T
    beta = _sigmoid(t[0:h])
    a = t[h:2 * h] + dtb_ref[...]
    softplus = jnp.maximum(a, 0.0) + jnp.log(1.0 + jnp.exp(-jnp.abs(a)))
    g = -jnp.exp(alog_ref[...]) * softplus
    r = lax.broadcasted_iota(jnp.int32, (tm, tm), 0)
    c = lax.broadcasted_iota(jnp.int32, (tm, tm), 1)
    shift = CHUNK.bit_length() - 1
    same = lax.shift_right_logical(r, shift) == lax.shift_right_logical(c, shift)
    cum_m = jnp.where(same & (r <= c), 1.0, 0.0).astype(BF16)
    tot_m = jnp.where(same, 1.0, 0.0).astype(BF16)
    parts = jnp.concatenate(_split3(g), axis=0).astype(BF16)
    cum3 = jnp.dot(parts, cum_m, preferred_element_type=F32)
    tot3 = jnp.dot(parts, tot_m, preferred_element_type=F32)
    gc = cum3[0:h] + cum3[h:2 * h] + cum3[2 * h:3 * h]
    gl = tot3[0:h] + tot3[h:2 * h] + tot3[2 * h:3 * h]
    o_ref[0:h, :] = beta
    o_ref[h:2 * h, :] = gc
    o_ref[2 * h:3 * h, :] = jnp.exp(gl - gc)
    o_ref[3 * h:4 * h, :] = jnp.exp(gl)


def _gates(p_small, a_log, dt_bias):
    n = p_small.shape[0]
    tm = 512
    h = GDN_HEADS
    return pl.pallas_call(
        _gate_body,
        out_shape=jax.ShapeDtypeStruct((4 * h, n), F32),
        grid=(n // tm,),
        in_specs=[pl.BlockSpec((tm, LANES), lambda i: (i, 1)),
                  pl.BlockSpec((h, 1), lambda i: (0, 0)),
                  pl.BlockSpec((h, 1), lambda i: (0, 0))],
        out_specs=pl.BlockSpec((4 * h, tm), lambda i: (0, i)),
        compiler_params=_params("arbitrary"),
        name="gdn_gates",
    )(p_small, a_log.reshape(h, 1), dt_bias.reshape(h, 1))


def _intra_body(q_ref, k_ref, v_ref, kt_ref, beta_ref, gc_ref, dec_ref,
                u_ref, w_ref, qe_ref, kdt_ref, attn_ref, *, nc):
    hh = pl.program_id(1)
    ri = lax.broadcasted_iota(jnp.int32, (CHUNK, CHUNK), 0)
    ci = lax.broadcasted_iota(jnp.int32, (CHUNK, CHUNK), 1)
    eye = ri == ci
    lower = ri >= ci
    strict = ri > ci
    kt = kt_ref[...]
    kdt_ref[...] = (kt.astype(F32) * dec_ref[pl.ds(hh, 1), :]).astype(BF16)

    def to_col(rowvec):
        return jnp.sum(jnp.where(eye, jnp.broadcast_to(rowvec, (CHUNK, CHUNK)), 0.0), axis=1, keepdims=True)

    cs = range(nc)
    rows = [slice(c * CHUNK, (c + 1) * CHUNK) for c in cs]
    g_row = [gc_ref[c:c + 1, :] for c in cs]
    g_col = [to_col(g) for g in g_row]
    b_col = [to_col(beta_ref[c:c + 1, :]) for c in cs]
    eg_col = [jnp.exp(g) for g in g_col]
    decay = [jnp.where(lower, jnp.exp(jnp.where(lower, gc - gr, 0.0)), 0.0) for gc, gr in zip(g_col, g_row)]
    ktc = [kt[:, r] for r in rows]
    qb = [q_ref[r, :] for r in rows]
    kb = [k_ref[r, :].astype(F32) * b for r, b in zip(rows, b_col)]
    rhs = [jnp.concatenate([v_ref[r, :].astype(F32) * b, k * e], axis=1)
           for r, b, k, e in zip(rows, b_col, kb, eg_col)]
    kk = [_bdot(k, t) for k, t in zip(kb, ktc)]
    qk = [_bdot(q, t) for q, t in zip(qb, ktc)]
    m1 = [-jnp.where(strict, k * d, 0.0) for k, d in zip(kk, decay)]
    m2 = [_bdot(m, m) for m in m1]
    m4 = [_bdot(m, m) for m in m2]
    x1 = [a + b + _bdot(a, b) for a, b in zip(m1, m2)]
    m8 = [_bdot(m, m) for m in m4]
    m16 = [_bdot(m, m) for m in m8]
    x2 = [a + b + _bdot(a, b) for a, b in zip(m4, m8)]
    m32 = [_bdot(m, m) for m in m16]
    x12 = [a + b + _bdot(a, b) for a, b in zip(x1, x2)]
    x3 = [a + b + _bdot(a, b) for a, b in zip(m16, m32)]
    xt = [a + b + _bdot(a, b) for a, b in zip(x12, x3)]
    sol = [r + _bdot(x, r) for x, r in zip(xt, rhs)]
    for c in cs:
        u_ref[rows[c], :] = sol[c][:, 0:GDN_DV].astype(BF16)
        w_ref[rows[c], :] = sol[c][:, GDN_DV:GDN_DV + GDN_DK].astype(BF16)
        qe_ref[rows[c], :] = (qb[c].astype(F32) * eg_col[c]).astype(BF16)
        attn_ref[rows[c], :] = jnp.where(lower, qk[c] * decay[c], 0.0).astype(BF16)


def _intra(qn, kn, vn, kt, gate_rows, gate_flat, batch, seq):
    n = qn.shape[0]
    h = GDN_HEADS
    tt = min(512, seq)
    nc = tt // CHUNK
    per_b = seq // tt
    wide = jax.ShapeDtypeStruct((n, h * LANES), BF16)
    row_spec = pl.BlockSpec((tt, LANES), lambda b, hh, t: (b * per_b + t, hh))
    return pl.pallas_call(
        functools.partial(_intra_body, nc=nc),
        out_shape=(wide, wide, wide,
                   jax.ShapeDtypeStruct((batch, h, LANES, seq), BF16),
                   jax.ShapeDtypeStruct((batch, h, seq, CHUNK), BF16)),
        grid=(batch, h, per_b),
        in_specs=[row_spec, row_spec, row_spec,
                  pl.BlockSpec((None, None, LANES, tt), lambda b, hh, t: (b, hh, 0, t)),
                  pl.BlockSpec((None, None, nc, CHUNK), lambda b, hh, t: (hh, b, t, 0)),
                  pl.BlockSpec((None, None, nc, CHUNK), lambda b, hh, t: (h + hh, b, t, 0)),
                  pl.BlockSpec((h, tt), lambda b, hh, t: (2, b * per_b + t))],
        out_specs=(row_spec, row_spec, row_spec,
                   pl.BlockSpec((None, None, LANES, tt), lambda b, hh, t: (b, hh, 0, t)),
                   pl.BlockSpec((None, None, tt, CHUNK), lambda b, hh, t: (b, hh, t, 0))),
        compiler_params=_params("arbitrary", "arbitrary", "arbitrary"),
        name="gdn_intra",
    )(qn, kn, vn, kt, gate_rows, gate_rows, gate_flat)


def _scan_body(u_ref, w_ref, qe_ref, kdt_ref, attn_ref, egl_ref, z_ref, gn_ref, o_ref, s_scr, *, nct):
    t = pl.program_id(1)

    @pl.when(t == 0)
    def _():
        s_scr[...] = jnp.zeros(s_scr.shape, F32)

    gn = gn_ref[...]
    hs = range(GDN_HEADS)
    cols = [slice(h * LANES, (h + 1) * LANES) for h in hs]
    state = [s_scr[h] for h in hs]
    for c in range(nct):
        rows = slice(c * CHUNK, (c + 1) * CHUNK)
        wq = [jnp.concatenate([w_ref[rows, cl], qe_ref[rows, cl]], axis=0) for cl in cols]
        r = [jnp.dot(a, s.astype(BF16), preferred_element_type=F32) for a, s in zip(wq, state)]
        v_new = [(u_ref[rows, cl].astype(F32) - x[0:CHUNK]).astype(BF16) for cl, x in zip(cols, r)]
        eg = [egl_ref[h, pl.ds(t * nct + c, 1), :][:, 0:1] for h in hs]
        state = [s * e + jnp.dot(kdt_ref[h, :, c * CHUNK:(c + 1) * CHUNK], v, preferred_element_type=F32)
                 for h, s, e, v in zip(hs, state, eg, v_new)]
        o = [x[CHUNK:2 * CHUNK] + jnp.dot(attn_ref[h, rows, :], v, preferred_element_type=F32)
             for h, x, v in zip(hs, r, v_new)]
        for h in hs:
            on = o[h] * lax.rsqrt(jnp.mean(o[h] * o[h], axis=-1, keepdims=True) + EPS) * gn
            o_ref[rows, cols[h]] = (on * _silu(z_ref[rows, cols[h]].astype(F32))).astype(BF16)
    for h in hs:
        s_scr[h] = state[h]


def _scan(u, w, qe, kdt, attn, egl, p_main, gdn_norm, batch, seq, z_blk):
    n = u.shape[0]
    h = GDN_HEADS
    tt = min(256, seq)
    nct = tt // CHUNK
    per_b = seq // tt
    wide_spec = pl.BlockSpec((tt, h * LANES), lambda b, t: (b * per_b + t, 0))
    return pl.pallas_call(
        functools.partial(_scan_body, nct=nct),
        out_shape=jax.ShapeDtypeStruct((n, h * LANES), BF16),
        grid=(batch, per_b),
        in_specs=[wide_spec, wide_spec, wide_spec,
                  pl.BlockSpec((None, h, LANES, tt), lambda b, t: (b, 0, 0, t)),
                  pl.BlockSpec((None, h, tt, CHUNK), lambda b, t: (b, 0, t, 0)),
                  pl.BlockSpec((h, None, seq // CHUNK, CHUNK), lambda b, t: (3, b, 0, 0)),
                  pl.BlockSpec((tt, h * LANES), lambda b, t: (b * per_b + t, z_blk)),
                  pl.BlockSpec((1, LANES), lambda b, t: (0, 0))],
        out_specs=wide_spec,
        scratch_shapes=[pltpu.VMEM((h, GDN_DK, GDN_DV), F32)],
        compiler_params=_params("arbitrary", "arbitrary"),
        name="gdn_scan",
    )(u, w, qe, kdt, attn, egl, p_main, gdn_norm)


def _merge_body(a_ref, b_ref, wa_ref, wb_ref, ga_ref, gb_ref, o_ref):
    ya = jnp.dot(a_ref[...], wa_ref[...], preferred_element_type=F32)
    yb = jnp.dot(b_ref[...], wb_ref[...], preferred_element_type=F32)
    o = _sigmoid(ga_ref[...].astype(F32)) * ya + _sigmoid(gb_ref[...].astype(F32)) * yb
    o_ref[...] = o.astype(o_ref.dtype)


def _merge(o_a, o_b, w_a, w_b, p_main, d):
    n, ka = o_a.shape
    kb = o_b.shape[1]
    tm, tn = 1024, 512
    nb = d // tn
    return pl.pallas_call(
        _merge_body,
        out_shape=jax.ShapeDtypeStruct((n, d), BF16),
        grid=(n // tm, nb),
        in_specs=[pl.BlockSpec((tm, ka), lambda i, j: (i, 0)),
                  pl.BlockSpec((tm, kb), lambda i, j: (i, 0)),
                  pl.BlockSpec((ka, tn), lambda i, j: (0, j)),
                  pl.BlockSpec((kb, tn), lambda i, j: (0, j)),
                  pl.BlockSpec((tm, tn), lambda i, j: (i, j)),
                  pl.BlockSpec((tm, tn), lambda i, j: (i, nb + j))],
        out_specs=pl.BlockSpec((tm, tn), lambda i, j: (i, j)),
        compiler_params=_params("arbitrary", "arbitrary"),
        name="branch_merge",
    )(o_a, o_b, w_a, w_b, p_main, p_main)


def _resid_body(a_ref, w_ref, x_ref, gt_ref, o_ref):
    y = jnp.dot(a_ref[...], w_ref[...], preferred_element_type=F32)
    o_ref[...] = x_ref[...] + gt_ref[...] * y


def _resid_matmul(a, w, x, gt, seq, tm, tn):
    n, k = a.shape
    d = w.shape[1]
    tm = min(tm, seq)
    per_b = seq // tm
    return pl.pallas_call(
        _resid_body,
        out_shape=jax.ShapeDtypeStruct((n, d), F32),
        grid=(n // tm, d // tn),
        in_specs=[pl.BlockSpec((tm, k), lambda i, j: (i, 0)),
                  pl.BlockSpec((k, tn), lambda i, j: (0, j)),
                  pl.BlockSpec((tm, tn), lambda i, j: (i, j)),
                  pl.BlockSpec((None, 1, tn), lambda i, j: (i // per_b, 0, j))],
        out_specs=pl.BlockSpec((tm, tn), lambda i, j: (i, j)),
        compiler_params=_params("arbitrary", "arbitrary"),
        name="resid_matmul",
    )(a, w, x, gt)


def _ffn_up_body(x_ref, nw_ref, sc_ref, sh_ref, wg_ref, wu_ref, o_ref, h_scr):
    @pl.when(pl.program_id(1) == 0)
    def _():
        h = _rms(x_ref[...], nw_ref[...]) * (1.0 + sc_ref[...]) + sh_ref[...]
        h_scr[...] = h.astype(BF16)

    hb = h_scr[...]
    gate = jnp.dot(hb, wg_ref[...], preferred_element_type=F32)
    up = jnp.dot(hb, wu_ref[...], preferred_element_type=F32)
    o_ref[...] = (_silu(gate) * up).astype(o_ref.dtype)


def _ffn_up(x, nw, sc, sh, w_gu, seq):
    n, d = x.shape
    dff = w_gu.shape[1] // 2
    tm, tn = min(1024, seq), 512
    per_b = seq // tm
    nb = dff // tn
    return pl.pallas_call(
        _ffn_up_body,
        out_shape=jax.ShapeDtypeStruct((n, dff), BF16),
        grid=(n // tm, nb),
        in_specs=[pl.BlockSpec((tm, d), lambda i, j: (i, 0)),
                  pl.BlockSpec((1, d), lambda i, j: (0, 0)),
                  pl.BlockSpec((None, 1, d), lambda i, j: (i // per_b, 0, 0)),
                  pl.BlockSpec((None, 1, d), lambda i, j: (i // per_b, 0, 0)),
                  pl.BlockSpec((d, tn), lambda i, j: (0, j)),
                  pl.BlockSpec((d, tn), lambda i, j: (0, nb + j))],
        out_specs=pl.BlockSpec((tm, tn), lambda i, j: (i, j)),
        scratch_shapes=[pltpu.VMEM((tm, d), BF16)],
        compiler_params=_params("arbitrary", "arbitrary"),
        name="ffn_up",
    )(x, nw, sc, sh, w_gu, w_gu)


def _final_norm_body(x_ref, w_ref, o_ref):
    o_ref[...] = _rms(x_ref[...], w_ref[...])


def _final_norm(x, w):
    n, d = x.shape
    tm = 512
    return pl.pallas_call(
        _final_norm_body,
        out_shape=jax.ShapeDtypeStruct((n, d), F32),
        grid=(n // tm,),
        in_specs=[pl.BlockSpec((tm, d), lambda i: (i, 0)), pl.BlockSpec((1, d), lambda i: (0, 0))],
        out_specs=pl.BlockSpec((tm, d), lambda i: (i, 0)),
        compiler_params=_params("arbitrary"),
        name="final_norm",
    )(x, w)


def _pad_rope_cols(w):
    half = QK_ROPE // 2
    z = jnp.zeros(w.shape[:-1] + (LANES // 2 - half,), w.dtype)
    return jnp.concatenate([w[..., :half], z, w[..., half:], z], axis=-1)


def _layout_w_in(w_in, d):
    o = 0
    cq = w_in[:, o:o + Q_LORA]; o += Q_LORA
    ckv = w_in[:, o:o + KV_LORA]; o += KV_LORA
    kpe = w_in[:, o:o + QK_ROPE]; o += QK_ROPE
    qkvz = w_in[:, o:o + 2 * GDN_QK + 2 * GDN_V]; o += 2 * GDN_QK + 2 * GDN_V
    ba = w_in[:, o:o + 2 * GDN_HEADS]; o += 2 * GDN_HEADS
    gates = w_in[:, o:o + 2 * d]
    w_main = jnp.concatenate([gates, qkvz, cq, ckv], axis=1).astype(BF16)
    pad = jnp.zeros((w_in.shape[0], LANES - 2 * GDN_HEADS), w_in.dtype)
    w_small = jnp.concatenate([_pad_rope_cols(kpe), ba, pad], axis=1).astype(BF16)
    return w_main, w_small


def _layout_w_uq(w_uq):
    k = w_uq.shape[0]
    w = w_uq.reshape(k, MLA_HEADS, QK_NOPE + QK_ROPE)
    w = jnp.concatenate([w[..., :QK_NOPE], _pad_rope_cols(w[..., QK_NOPE:])], axis=-1)
    return w.reshape(k, MLA_HEADS * MLA_QK_PAD).astype(BF16)


def _rope_tables(positions):
    inv_freq = 1.0 / (ROPE_THETA ** (jnp.arange(0, QK_ROPE, 2, dtype=F32) / QK_ROPE))
    ang = positions.astype(F32).reshape(-1)[:, None] * inv_freq
    cos, sin = jnp.cos(ang), jnp.sin(ang)
    z = jnp.zeros_like(cos)
    return jnp.concatenate([cos, z, cos, z], axis=1), jnp.concatenate([-sin, z, sin, z], axis=1)


def kernel(x, c, positions, w_ada, b_ada, norm_mix, norm_ffn, w_in, q_a_norm, kv_a_norm, w_uq, w_ukv, w_o_mla,
           conv_w, A_log, dt_bias, gdn_norm, w_o_gdn, w_o, w_gate_up, w_down, final_norm):
    batch, seq, d = x.shape
    depth = w_ada.shape[0]
    n = batch * seq
    h = GDN_HEADS
    cos_t, sin_t = _rope_tables(positions)
    mod = _ada(c, w_ada, b_ada).reshape(depth, batch, 6, 1, d)
    xs = x.reshape(n, d)

    qkv_blk0 = 2 * d // LANES
    z_blk = (2 * d + 2 * GDN_QK + GDN_V) // (h * LANES)
    cq_blk = (2 * d + 2 * GDN_QK + 2 * GDN_V) // Q_LORA
    ckv_blk = cq_blk + 1

    for l in range(depth):
        sh_a, sc_a, gt_a, sh_f, sc_f, gt_f = [mod[l, :, i] for i in range(6)]
        w_main, w_small = _layout_w_in(w_in[l], d)
        p_main, p_small = _inproj(xs, norm_mix[l].reshape(1, d), sc_a, sh_a, w_main, w_small, seq)

        q, kv, kpe = _mla_proj(p_main, p_small, cos_t, sin_t, q_a_norm[l].reshape(1, -1),
                               kv_a_norm[l].reshape(1, -1), _layout_w_uq(w_uq[l]), w_ukv[l].astype(BF16),
                               cq_blk, ckv_blk)
        o_a = _flash(q.reshape(batch, seq, -1), kv.reshape(batch, seq, -1), kpe.reshape(batch, seq, -1))
        o_a = o_a.reshape(n, -1)

        (qn,) = _conv(p_main, conv_w[l], batch, seq, qkv_blk0, 0, GDN_DK ** -0.5, False)
        kn, kt = _conv(p_main, conv_w[l], batch, seq, qkv_blk0 + h, h, 1.0, True)
        (vn,) = _conv(p_main, conv_w[l], batch, seq, qkv_blk0 + 2 * h, 2 * h, None, False)
        gate_flat = _gates(p_small, A_log[l], dt_bias[l])
        gate_rows = gate_flat.reshape(4 * h, batch, seq // CHUNK, CHUNK)
        u, w, qe, kdt, attn = _intra(qn, kn, vn, kt, gate_rows, gate_flat, batch, seq)
        o_b = _scan(u, w, qe, kdt, attn, gate_rows, p_main, gdn_norm[l].reshape(1, -1), batch, seq, z_blk)

        merged = _merge(o_a, o_b, w_o_mla[l].astype(BF16), w_o_gdn[l].astype(BF16), p_main, d)
        xs = _resid_matmul(merged, w_o[l].astype(BF16), xs, gt_a, seq, 1024, 512)

        act = _ffn_up(xs, norm_ffn[l].reshape(1, d), sc_f, sh_f, w_gate_up[l].astype(BF16), seq)
        xs = _resid_matmul(act, w_down[l].astype(BF16), xs, gt_f, seq, 1024, 256)

    return _final_norm(xs, final_norm.reshape(1, d)).reshape(batch, seq, d)
```

```python
import functools
import math

import jax
import jax.numpy as jnp
from jax import lax
from jax.experimental import pallas as pl
from jax.experimental.pallas import tpu as pltpu

F32 = jnp.float32
BF16 = jnp.bfloat16

MLA_HEADS = 8
QK_NOPE = 128
QK_ROPE = 64
V_HEAD = 128
Q_LORA = 512
KV_LORA = 512
ROPE_THETA = 10000.0
GDN_HEADS = 8
GDN_DK = 128
GDN_DV = 128
CONV_WIDTH = 4
CHUNK = 64
EPS = 1e-6

LANES = 128
MLA_QK_PAD = 2 * LANES
VMEM_LIMIT_BYTES = 56 * 1024 * 1024

GDN_QK = GDN_HEADS * GDN_DK
GDN_V = GDN_HEADS * GDN_DV


def _params(*semantics):
    return pltpu.CompilerParams(dimension_semantics=semantics, vmem_limit_bytes=VMEM_LIMIT_BYTES)


def _sigmoid(x):
    return 1.0 / (1.0 + jnp.exp(-x))


def _silu(x):
    return x * _sigmoid(x)


def _rms(x, w):
    return x * lax.rsqrt(jnp.mean(x * x, axis=-1, keepdims=True) + EPS) * w


def _bdot(a, b):
    return jnp.dot(a.astype(BF16), b.astype(BF16), preferred_element_type=F32)


def _ada_body(c_ref, w_ref, b_ref, o_ref):
    c = c_ref[...]
    o_ref[...] = _bdot(_silu(c), w_ref[...]) + b_ref[...]


def _ada(c, w_ada, b_ada):
    depth, d, n6 = w_ada.shape
    b = c.shape[0]
    tn = 1024
    return pl.pallas_call(
        _ada_body,
        out_shape=jax.ShapeDtypeStruct((depth, b, n6), F32),
        grid=(depth, n6 // tn),
        in_specs=[pl.BlockSpec((b, d), lambda l, j: (0, 0)),
                  pl.BlockSpec((None, d, tn), lambda l, j: (l, 0, j)),
                  pl.BlockSpec((None, 1, tn), lambda l, j: (l, 0, j))],
        out_specs=pl.BlockSpec((None, b, tn), lambda l, j: (l, 0, j)),
        compiler_params=_params("arbitrary", "arbitrary"),
        name="ada_mod",
    )(c, w_ada, b_ada.reshape(depth, 1, n6))


def _inproj_body(x_ref, nw_ref, sc_ref, sh_ref, w_ref, ws_ref, o_ref, os_ref, h_scr):
    @pl.when(pl.program_id(1) == 0)
    def _():
        h = _rms(x_ref[...], nw_ref[...]) * (1.0 + sc_ref[...]) + sh_ref[...]
        hb = h.astype(BF16)
        h_scr[...] = hb
        os_ref[...] = jnp.dot(hb, ws_ref[...], preferred_element_type=F32)

    o_ref[...] = jnp.dot(h_scr[...], w_ref[...], preferred_element_type=F32).astype(o_ref.dtype)


def _inproj(x, nw, sc, sh, w_main, w_small, layer, seq):
    n, d = x.shape
    nm = w_main.shape[-1]
    ns = w_small.shape[-1]
    tm, tn = min(1024, seq), 1024
    per_b = seq // tm
    return pl.pallas_call(
        _inproj_body,
        out_shape=(jax.ShapeDtypeStruct((n, nm), BF16), jax.ShapeDtypeStruct((n, ns), F32)),
        grid=(n // tm, nm // tn),
        in_specs=[pl.BlockSpec((tm, d), lambda i, j: (i, 0)),
                  pl.BlockSpec((1, d), lambda i, j: (0, 0)),
                  pl.BlockSpec((None, 1, d), lambda i, j: (i // per_b, 0, 0)),
                  pl.BlockSpec((None, 1, d), lambda i, j: (i // per_b, 0, 0)),
                  pl.BlockSpec((None, d, tn), lambda i, j: (layer, 0, j)),
                  pl.BlockSpec((None, d, ns), lambda i, j: (layer, 0, 0))],
        out_specs=(pl.BlockSpec((tm, tn), lambda i, j: (i, j)),
                   pl.BlockSpec((tm, ns), lambda i, j: (i, 0))),
        scratch_shapes=[pltpu.VMEM((tm, d), BF16)],
        compiler_params=_params("arbitrary", "arbitrary"),
        name="in_proj",
    )(x, nw, sc, sh, w_main, w_small)


def _rope128(x, cos, sin):
    return x * cos + pltpu.roll(x, LANES // 2, 1) * sin


def _mla_proj_body(cq_ref, ckv_ref, kpe_ref, cos_ref, sin_ref, qn_ref, kvn_ref, wq_ref, wkv_ref,
                   q_ref, kv_ref, kpeo_ref):
    cos = cos_ref[...]
    sin = sin_ref[...]
    scale = (QK_NOPE + QK_ROPE) ** -0.5 * math.log2(math.e)
    q = _bdot(_rms(cq_ref[...].astype(F32), qn_ref[...]), wq_ref[...])
    for h in range(MLA_HEADS):
        lo = h * MLA_QK_PAD
        q_ref[:, lo:lo + LANES] = (q[:, lo:lo + LANES] * scale).astype(BF16)
        pe = _rope128(q[:, lo + LANES:lo + 2 * LANES], cos, sin)
        q_ref[:, lo + LANES:lo + 2 * LANES] = (pe * scale).astype(BF16)
    kv_ref[...] = _bdot(_rms(ckv_ref[...].astype(F32), kvn_ref[...]), wkv_ref[...]).astype(BF16)
    kpeo_ref[...] = _rope128(kpe_ref[...], cos, sin).astype(BF16)


def _mla_proj(p_main, p_small, cos_t, sin_t, qn, kvn, wq, wkv, layer, cq_blk, ckv_blk):
    n = p_main.shape[0]
    tm = 512
    nq, nkv = wq.shape[-1], wkv.shape[-1]
    return pl.pallas_call(
        _mla_proj_body,
        out_shape=(jax.ShapeDtypeStruct((n, nq), BF16), jax.ShapeDtypeStruct((n, nkv), BF16),
                   jax.ShapeDtypeStruct((n, LANES), BF16)),
        grid=(n // tm,),
        in_specs=[pl.BlockSpec((tm, Q_LORA), lambda i: (i, cq_blk)),
                  pl.BlockSpec((tm, KV_LORA), lambda i: (i, ckv_blk)),
                  pl.BlockSpec((tm, LANES), lambda i: (i, 0)),
                  pl.BlockSpec((tm, LANES), lambda i: (i, 0)),
                  pl.BlockSpec((tm, LANES), lambda i: (i, 0)),
                  pl.BlockSpec((1, Q_LORA), lambda i: (0, 0)),
                  pl.BlockSpec((1, KV_LORA), lambda i: (0, 0)),
                  pl.BlockSpec((None, Q_LORA, nq), lambda i: (layer, 0, 0)),
                  pl.BlockSpec((None, KV_LORA, nkv), lambda i: (layer, 0, 0))],
        out_specs=(pl.BlockSpec((tm, nq), lambda i: (i, 0)),
                   pl.BlockSpec((tm, nkv), lambda i: (i, 0)),
                   pl.BlockSpec((tm, LANES), lambda i: (i, 0))),
        compiler_params=_params("arbitrary"),
        name="mla_proj",
    )(p_main, p_main, p_small, cos_t, sin_t, qn, kvn, wq, wkv)


def _flash_body(q_ref, kv_ref, kpe_ref, o_ref, *, tq, nq):
    tiles = [slice(i * tq, (i + 1) * tq) for i in range(nq)]
    row = lax.broadcasted_iota(jnp.int32, (tq, tq), 0)
    col = lax.broadcasted_iota(jnp.int32, (tq, tq), 1)
    causal = row >= col

    def scores(r):
        k = jnp.concatenate([kv_ref[tiles[r], 0:QK_NOPE], kpe_ref[tiles[r], :]], axis=1)
        return [lax.dot_general(q_ref[tiles[i], :], k, (((1,), (1,)), ((), ())), preferred_element_type=F32)
                for i in range(r, nq)]

    m = [None] * nq
    l = [None] * nq
    acc = [None] * nq
    s_next = scores(0)
    for r in range(nq):
        s_cur = s_next
        if r + 1 < nq:
            s_next = scores(r + 1)
        v = kv_ref[tiles[r], QK_NOPE:QK_NOPE + V_HEAD]
        p = []
        alpha = []
        for i, s in zip(range(r, nq), s_cur):
            if i == r:
                s = jnp.where(causal, s, -jnp.inf)
            s_max = jnp.max(s, axis=-1, keepdims=True)
            m_new = s_max if r == 0 else jnp.maximum(m[i], s_max)
            e = jnp.exp2(s - m_new)
            e_sum = jnp.sum(e, axis=-1, keepdims=True)
            if r == 0:
                alpha.append(None)
                l[i] = e_sum
            else:
                a = jnp.exp2(m[i] - m_new)
                alpha.append(a)
                l[i] = a * l[i] + e_sum
            m[i] = m_new
            p.append(e.astype(BF16))
        for i, pi, a in zip(range(r, nq), p, alpha):
            pv = jnp.dot(pi, v, preferred_element_type=F32)
            acc[i] = pv if a is None else a * acc[i] + pv
        o_ref[tiles[r], :] = (acc[r] / l[r]).astype(o_ref.dtype)


def _flash(q, kv, kpe):
    b, t, _ = q.shape
    tq = min(512, t)
    return pl.pallas_call(
        functools.partial(_flash_body, tq=tq, nq=t // tq),
        out_shape=jax.ShapeDtypeStruct((b, t, MLA_HEADS * V_HEAD), BF16),
        grid=(b, MLA_HEADS),
        in_specs=[pl.BlockSpec((None, t, MLA_QK_PAD), lambda bi, h: (bi, 0, h)),
                  pl.BlockSpec((None, t, QK_NOPE + V_HEAD), lambda bi, h: (bi, 0, h)),
                  pl.BlockSpec((None, t, LANES), lambda bi, h: (bi, 0, 0))],
        out_specs=pl.BlockSpec((None, t, V_HEAD), lambda bi, h: (bi, 0, h)),
        compiler_params=_params("arbitrary", "arbitrary"),
        name="mla_flash",
    )(q, kv, kpe)


def _conv_silu(x_ref, w_ref):
    u = x_ref[...].astype(F32)
    w = w_ref[...]
    sub = 8
    row = lax.broadcasted_iota(jnp.int32, (sub, u.shape[1]), 0)
    y = u * w[CONV_WIDTH - 1:CONV_WIDTH, :]
    for s in range(1, CONV_WIDTH):
        rolled = pltpu.roll(u, s, 0)
        shifted = jnp.concatenate([jnp.where(row >= s, rolled[0:sub], 0.0), rolled[sub:]], axis=0)
        y = y + shifted * w[CONV_WIDTH - 1 - s:CONV_WIDTH - s, :]
    return _silu(y)


def _l2norm(y):
    return y * lax.rsqrt(jnp.sum(y * y, axis=-1, keepdims=True) + EPS)


def _conv_body(xq_ref, xk_ref, xv_ref, wq_ref, wk_ref, wv_ref, q_ref, k_ref, v_ref, kt_ref):
    q_ref[...] = (_l2norm(_conv_silu(xq_ref, wq_ref)) * GDN_DK ** -0.5).astype(BF16)
    k = _l2norm(_conv_silu(xk_ref, wk_ref))
    k_ref[...] = k.astype(BF16)
    kt_ref[...] = k.T.astype(BF16)
    v_ref[...] = _conv_silu(xv_ref, wv_ref).astype(BF16)


def _conv(p_main, conv_w, batch, seq, col_blk0):
    n = p_main.shape[0]
    h = GDN_HEADS
    wide = jax.ShapeDtypeStruct((n, h * LANES), BF16)
    out_spec = pl.BlockSpec((seq, LANES), lambda b, hh: (b, hh))

    def x_spec(kind):
        return pl.BlockSpec((seq, LANES), lambda b, hh: (b, col_blk0 + kind * h + hh))

    def w_spec(kind):
        return pl.BlockSpec((CONV_WIDTH, LANES), lambda b, hh: (0, kind * h + hh))

    return pl.pallas_call(
        _conv_body,
        out_shape=(wide, wide, wide, jax.ShapeDtypeStruct((batch, h, LANES, seq), BF16)),
        grid=(batch, h),
        in_specs=[x_spec(0), x_spec(1), x_spec(2), w_spec(0), w_spec(1), w_spec(2)],
        out_specs=(out_spec, out_spec, out_spec,
                   pl.BlockSpec((None, None, LANES, seq), lambda b, hh: (b, hh, 0, 0))),
        compiler_params=_params("arbitrary", "arbitrary"),
        name="gdn_conv",
    )(p_main, p_main, p_main, conv_w, conv_w, conv_w)


def _split3(x):
    hi = x.astype(BF16).astype(F32)
    mid = (x - hi).astype(BF16).astype(F32)
    lo = x - hi - mid
    return hi, mid, lo


def _gate_body(s_ref, alog_ref, dtb_ref, o_ref):
    tm = s_ref.shape[0]
    h = GDN_HEADS
    t = s_ref[...].T
    beta = _sigmoid(t[0:h])
    a = t[h:2 * h] + dtb_ref[...]
    softplus = jnp.maximum(a, 0.0) + jnp.log(1.0 + jnp.exp(-jnp.abs(a)))
    g = -jnp.exp(alog_ref[...]) * softplus
    r = lax.broadcasted_iota(jnp.int32, (tm, tm), 0)
    c = lax.broadcasted_iota(jnp.int32, (tm, tm), 1)
    shift = CHUNK.bit_length() - 1
    same = lax.shift_right_logical(r, shift) == lax.shift_right_logical(c, shift)
    cum_m = jnp.where(same & (r <= c), 1.0, 0.0).astype(BF16)
    tot_m = jnp.where(same, 1.0, 0.0).astype(BF16)
    parts = jnp.concatenate(_split3(g), axis=0).astype(BF16)
    cum3 = jnp.dot(parts, cum_m, preferred_element_type=F32)
    tot3 = jnp.dot(parts, tot_m, preferred_element_type=F32)
    gc = cum3[0:h] + cum3[h:2 * h] + cum3[2 * h:3 * h]
    gl = tot3[0:h] + tot3[h:2 * h] + tot3[2 * h:3 * h]
    o_ref[0:h, :] = beta
    o_ref[h:2 * h, :] = gc
    o_ref[2 * h:3 * h, :] = jnp.exp(gl - gc)
    o_ref[3 * h:4 * h, :] = jnp.exp(gl)


def _gates(p_small, a_log, dt_bias):
    n = p_small.shape[0]
    tm = 512
    h = GDN_HEADS
    return pl.pallas_call(
        _gate_body,
        out_shape=jax.ShapeDtypeStruct((4 * h, n), F32),
        grid=(n // tm,),
        in_specs=[pl.BlockSpec((tm, LANES), lambda i: (i, 1)),
                  pl.BlockSpec((h, 1), lambda i: (0, 0)),
                  pl.BlockSpec((h, 1), lambda i: (0, 0))],
        out_specs=pl.BlockSpec((4 * h, tm), lambda i: (0, i)),
        compiler_params=_params("arbitrary"),
        name="gdn_gates",
    )(p_small, a_log.reshape(h, 1), dt_bias.reshape(h, 1))


def _intra_body(q_ref, k_ref, v_ref, kt_ref, beta_ref, gc_ref, dec_ref,
                u_ref, w_ref, qe_ref, kdt_ref, attn_ref, *, nc):
    hh = pl.program_id(1)
    ri = lax.broadcasted_iota(jnp.int32, (CHUNK, CHUNK), 0)
    ci = lax.broadcasted_iota(jnp.int32, (CHUNK, CHUNK), 1)
    eye = ri == ci
    lower = ri >= ci
    strict = ri > ci
    kt = kt_ref[...]
    kdt_ref[...] = (kt.astype(F32) * dec_ref[pl.ds(hh, 1), :]).astype(BF16)

    def to_col(rowvec):
        return jnp.sum(jnp.where(eye, jnp.broadcast_to(rowvec, (CHUNK, CHUNK)), 0.0), axis=1, keepdims=True)

    cs = range(nc)
    rows = [slice(c * CHUNK, (c + 1) * CHUNK) for c in cs]
    g_row = [gc_ref[c:c + 1, :] for c in cs]
    g_col = [to_col(g) for g in g_row]
    b_col = [to_col(beta_ref[c:c + 1, :]) for c in cs]
    eg_col = [jnp.exp(g) for g in g_col]
    decay = [jnp.where(lower, jnp.exp(jnp.where(lower, gc - gr, 0.0)), 0.0) for gc, gr in zip(g_col, g_row)]
    ktc = [kt[:, r] for r in rows]
    qb = [q_ref[r, :] for r in rows]
    kb = [k_ref[r, :].astype(F32) * b for r, b in zip(rows, b_col)]
    rhs = [jnp.concatenate([v_ref[r, :].astype(F32) * b, k * e], axis=1)
           for r, b, k, e in zip(rows, b_col, kb, eg_col)]
    kk = [_bdot(k, t) for k, t in zip(kb, ktc)]
    qk = [_bdot(q, t) for q, t in zip(qb, ktc)]
    lmat = [jnp.where(strict, k * d, 0.0) for k, d in zip(kk, decay)]
    xt = None
    s = 1
    while s < CHUNK:
        shift = s.bit_length() - 1
        bi = lax.shift_right_logical(ri, shift)
        bj = lax.shift_right_logical(ci, shift)
        pair = (lax.shift_right_logical(bi, 1) == lax.shift_right_logical(bj, 1)) & ((bi & 1) == 1) & ((bj & 1) == 0)
        cblk = [jnp.where(pair, m, 0.0) for m in lmat]
        if xt is None:
            xt = [-c for c in cblk]
        else:
            y = [c + _bdot(x, c) for x, c in zip(xt, cblk)]
            xt = [x - (a + _bdot(a, x)) for x, a in zip(xt, y)]
        s *= 2
    sol = [r + _bdot(x, r) for x, r in zip(xt, rhs)]
    for c in cs:
        u_ref[rows[c], :] = sol[c][:, 0:GDN_DV].astype(BF16)
        w_ref[rows[c], :] = sol[c][:, GDN_DV:GDN_DV + GDN_DK].astype(BF16)
        qe_ref[rows[c], :] = (qb[c].astype(F32) * eg_col[c]).astype(BF16)
        attn_ref[rows[c], :] = jnp.where(lower, qk[c] * decay[c], 0.0).astype(BF16)


def _intra(qn, kn, vn, kt, gate_rows, gate_flat, batch, seq):
    n = qn.shape[0]
    h = GDN_HEADS
    tt = min(2048, seq)
    nc = tt // CHUNK
    per_b = seq // tt
    wide = jax.ShapeDtypeStruct((n, h * LANES), BF16)
    row_spec = pl.BlockSpec((tt, LANES), lambda b, hh, t: (b * per_b + t, hh))
    return pl.pallas_call(
        functools.partial(_intra_body, nc=nc),
        out_shape=(wide, wide, wide,
                   jax.ShapeDtypeStruct((batch, h, LANES, seq), BF16),
                   jax.ShapeDtypeStruct((batch, h, seq, CHUNK), BF16)),
        grid=(batch, h, per_b),
        in_specs=[row_spec, row_spec, row_spec,
                  pl.BlockSpec((None, None, LANES, tt), lambda b, hh, t: (b, hh, 0, t)),
                  pl.BlockSpec((None, None, nc, CHUNK), lambda b, hh, t: (hh, b, t, 0)),
                  pl.BlockSpec((None, None, nc, CHUNK), lambda b, hh, t: (h + hh, b, t, 0)),
                  pl.BlockSpec((h, tt), lambda b, hh, t: (2, b * per_b + t))],
        out_specs=(row_spec, row_spec, row_spec,
                   pl.BlockSpec((None, None, LANES, tt), lambda b, hh, t: (b, hh, 0, t)),
                   pl.BlockSpec((None, None, tt, CHUNK), lambda b, hh, t: (b, hh, t, 0))),
        compiler_params=_params("arbitrary", "arbitrary", "arbitrary"),
        name="gdn_intra",
    )(qn, kn, vn, kt, gate_rows, gate_rows, gate_flat)


def _scan_body(u_ref, w_ref, qe_ref, kdt_ref, attn_ref, egl_ref, z_ref, gn_ref, o_ref, s_scr, *, nct):
    t = pl.program_id(1)

    @pl.when(t == 0)
    def _():
        s_scr[...] = jnp.zeros(s_scr.shape, F32)

    gn = gn_ref[...]
    hs = range(GDN_HEADS)
    cols = [slice(h * LANES, (h + 1) * LANES) for h in hs]
    state = [s_scr[h] for h in hs]
    for c in range(nct):
        rows = slice(c * CHUNK, (c + 1) * CHUNK)
        wq = [jnp.concatenate([w_ref[rows, cl], qe_ref[rows, cl]], axis=0) for cl in cols]
        r = [jnp.dot(a, s.astype(BF16), preferred_element_type=F32) for a, s in zip(wq, state)]
        v_new = [(u_ref[rows, cl].astype(F32) - x[0:CHUNK]).astype(BF16) for cl, x in zip(cols, r)]
        eg = [egl_ref[h, pl.ds(t * nct + c, 1), :][:, 0:1] for h in hs]
        state = [s * e + jnp.dot(kdt_ref[h, :, c * CHUNK:(c + 1) * CHUNK], v, preferred_element_type=F32)
                 for h, s, e, v in zip(hs, state, eg, v_new)]
        o = [x[CHUNK:2 * CHUNK] + jnp.dot(attn_ref[h, rows, :], v, preferred_element_type=F32)
             for h, x, v in zip(hs, r, v_new)]
        for h in hs:
            on = o[h] * lax.rsqrt(jnp.mean(o[h] * o[h], axis=-1, keepdims=True) + EPS) * gn
            o_ref[rows, cols[h]] = (on * _silu(z_ref[rows, cols[h]].astype(F32))).astype(BF16)
    for h in hs:
        s_scr[h] = state[h]


def _scan(u, w, qe, kdt, attn, egl, p_main, gdn_norm, batch, seq, z_blk):
    n = u.shape[0]
    h = GDN_HEADS
    tt = min(256, seq)
    nct = tt // CHUNK
    per_b = seq // tt
    wide_spec = pl.BlockSpec((tt, h * LANES), lambda b, t: (b * per_b + t, 0))
    return pl.pallas_call(
        functools.partial(_scan_body, nct=nct),
        out_shape=jax.ShapeDtypeStruct((n, h * LANES), BF16),
        grid=(batch, per_b),
        in_specs=[wide_spec, wide_spec, wide_spec,
                  pl.BlockSpec((None, h, LANES, tt), lambda b, t: (b, 0, 0, t)),
                  pl.BlockSpec((None, h, tt, CHUNK), lambda b, t: (b, 0, t, 0)),
                  pl.BlockSpec((h, None, seq // CHUNK, CHUNK), lambda b, t: (3, b, 0, 0)),
                  pl.BlockSpec((tt, h * LANES), lambda b, t: (b * per_b + t, z_blk)),
                  pl.BlockSpec((1, LANES), lambda b, t: (0, 0))],
        out_specs=wide_spec,
        scratch_shapes=[pltpu.VMEM((h, GDN_DK, GDN_DV), F32)],
        compiler_params=_params("arbitrary", "arbitrary"),
        name="gdn_scan",
    )(u, w, qe, kdt, attn, egl, p_main, gdn_norm)


def _merge_body(a_ref, b_ref, wa_ref, wb_ref, ga_ref, gb_ref, o_ref):
    ya = jnp.dot(a_ref[...], wa_ref[...], preferred_element_type=F32)
    yb = jnp.dot(b_ref[...], wb_ref[...], preferred_element_type=F32)
    o = _sigmoid(ga_ref[...].astype(F32)) * ya + _sigmoid(gb_ref[...].astype(F32)) * yb
    o_ref[...] = o.astype(o_ref.dtype)


def _merge(o_a, o_b, w_a, w_b, layer, p_main, d):
    n, ka = o_a.shape
    kb = o_b.shape[1]
    tm, tn = 1024, 1024
    nb = d // tn
    return pl.pallas_call(
        _merge_body,
        out_shape=jax.ShapeDtypeStruct((n, d), BF16),
        grid=(n // tm, nb),
        in_specs=[pl.BlockSpec((tm, ka), lambda i, j: (i, 0)),
                  pl.BlockSpec((tm, kb), lambda i, j: (i, 0)),
                  pl.BlockSpec((None, ka, tn), lambda i, j: (layer, 0, j)),
                  pl.BlockSpec((None, kb, tn), lambda i, j: (layer, 0, j)),
                  pl.BlockSpec((tm, tn), lambda i, j: (i, j)),
                  pl.BlockSpec((tm, tn), lambda i, j: (i, nb + j))],
        out_specs=pl.BlockSpec((tm, tn), lambda i, j: (i, j)),
        compiler_params=_params("arbitrary", "arbitrary"),
        name="branch_merge",
    )(o_a, o_b, w_a, w_b, p_main, p_main)


def _resid_body(a_ref, w_ref, x_ref, gt_ref, o_ref):
    y = jnp.dot(a_ref[...], w_ref[...], preferred_element_type=F32)
    o_ref[...] = x_ref[...] + gt_ref[...] * y


def _resid_matmul(a, w, layer, x, gt, seq, tm, tn):
    n, k = a.shape
    d = w.shape[-1]
    tm = min(tm, seq)
    per_b = seq // tm
    return pl.pallas_call(
        _resid_body,
        out_shape=jax.ShapeDtypeStruct((n, d), F32),
        grid=(n // tm, d // tn),
        in_specs=[pl.BlockSpec((tm, k), lambda i, j: (i, 0)),
                  pl.BlockSpec((None, k, tn), lambda i, j: (layer, 0, j)),
                  pl.BlockSpec((tm, tn), lambda i, j: (i, j)),
                  pl.BlockSpec((None, 1, tn), lambda i, j: (i // per_b, 0, j))],
        out_specs=pl.BlockSpec((tm, tn), lambda i, j: (i, j)),
        compiler_params=_params("arbitrary", "arbitrary"),
        name="resid_matmul",
    )(a, w, x, gt)


def _ffn_up_body(x_ref, nw_ref, sc_ref, sh_ref, wg_ref, wu_ref, o_ref, h_scr):
    @pl.when(pl.program_id(1) == 0)
    def _():
        h = _rms(x_ref[...], nw_ref[...]) * (1.0 + sc_ref[...]) + sh_ref[...]
        h_scr[...] = h.astype(BF16)

    hb = h_scr[...]
    gate = jnp.dot(hb, wg_ref[...], preferred_element_type=F32)
    up = jnp.dot(hb, wu_ref[...], preferred_element_type=F32)
    o_ref[...] = (_silu(gate) * up).astype(o_ref.dtype)


def _ffn_up(x, nw, sc, sh, w_gu, layer, seq):
    n, d = x.shape
    dff = w_gu.shape[-1] // 2
    tm, tn = min(1024, seq), 512
    per_b = seq // tm
    nb = dff // tn
    return pl.pallas_call(
        _ffn_up_body,
        out_shape=jax.ShapeDtypeStruct((n, dff), BF16),
        grid=(n // tm, nb),
        in_specs=[pl.BlockSpec((tm, d), lambda i, j: (i, 0)),
                  pl.BlockSpec((1, d), lambda i, j: (0, 0)),
                  pl.BlockSpec((None, 1, d), lambda i, j: (i // per_b, 0, 0)),
                  pl.BlockSpec((None, 1, d), lambda i, j: (i // per_b, 0, 0)),
                  pl.BlockSpec((None, d, tn), lambda i, j: (layer, 0, j)),
                  pl.BlockSpec((None, d, tn), lambda i, j: (layer, 0, nb + j))],
        out_specs=pl.BlockSpec((tm, tn), lambda i, j: (i, j)),
        scratch_shapes=[pltpu.VMEM((tm, d), BF16)],
        compiler_params=_params("arbitrary", "arbitrary"),
        name="ffn_up",
    )(x, nw, sc, sh, w_gu, w_gu)


def _final_norm_body(x_ref, w_ref, o_ref):
    o_ref[...] = _rms(x_ref[...], w_ref[...])


def _final_norm(x, w):
    n, d = x.shape
    tm = 512
    return pl.pallas_call(
        _final_norm_body,
        out_shape=jax.ShapeDtypeStruct((n, d), F32),
        grid=(n // tm,),
        in_specs=[pl.BlockSpec((tm, d), lambda i: (i, 0)), pl.BlockSpec((1, d), lambda i: (0, 0))],
        out_specs=pl.BlockSpec((tm, d), lambda i: (i, 0)),
        compiler_params=_params("arbitrary"),
        name="final_norm",
    )(x, w)


def _pad_rope_cols(w):
    half = QK_ROPE // 2
    z = jnp.zeros(w.shape[:-1] + (LANES // 2 - half,), w.dtype)
    return jnp.concatenate([w[..., :half], z, w[..., half:], z], axis=-1)


def _layout_w_in(w_in, d):
    w_in = w_in.astype(BF16)
    o = 0
    cq_ckv = w_in[..., o:o + Q_LORA + KV_LORA]; o += Q_LORA + KV_LORA
    kpe = w_in[..., o:o + QK_ROPE]; o += QK_ROPE
    qkvz = w_in[..., o:o + 2 * GDN_QK + 2 * GDN_V]; o += 2 * GDN_QK + 2 * GDN_V
    ba = w_in[..., o:o + 2 * GDN_HEADS]; o += 2 * GDN_HEADS
    gates = w_in[..., o:o + 2 * d]
    w_main = jnp.concatenate([gates, qkvz, cq_ckv], axis=-1)
    pad = jnp.zeros(w_in.shape[:-1] + (LANES - 2 * GDN_HEADS,), BF16)
    w_small = jnp.concatenate([_pad_rope_cols(kpe), ba, pad], axis=-1)
    return w_main, w_small


def _layout_w_uq(w_uq):
    lead = w_uq.shape[:-1]
    w = w_uq.astype(BF16).reshape(lead + (MLA_HEADS, QK_NOPE + QK_ROPE))
    w = jnp.concatenate([w[..., :QK_NOPE], _pad_rope_cols(w[..., QK_NOPE:])], axis=-1)
    return w.reshape(lead + (MLA_HEADS * MLA_QK_PAD,))


def _rope_tables(positions):
    inv_freq = 1.0 / (ROPE_THETA ** (jnp.arange(0, QK_ROPE, 2, dtype=F32) / QK_ROPE))
    ang = positions.astype(F32).reshape(-1)[:, None] * inv_freq
    cos, sin = jnp.cos(ang), jnp.sin(ang)
    z = jnp.zeros_like(cos)
    return jnp.concatenate([cos, z, cos, z], axis=1), jnp.concatenate([-sin, z, sin, z], axis=1)


def kernel(x, c, positions, w_ada, b_ada, norm_mix, norm_ffn, w_in, q_a_norm, kv_a_norm, w_uq, w_ukv, w_o_mla,
           conv_w, A_log, dt_bias, gdn_norm, w_o_gdn, w_o, w_gate_up, w_down, final_norm):
    batch, seq, d = x.shape
    depth = w_ada.shape[0]
    n = batch * seq
    h = GDN_HEADS
    cos_t, sin_t = _rope_tables(positions)
    mod = _ada(c, w_ada, b_ada).reshape(depth, batch, 6, 1, d)
    xs = x.reshape(n, d)

    qkv_blk0 = 2 * d // LANES
    z_blk = (2 * d + 2 * GDN_QK + GDN_V) // (h * LANES)
    cq_blk = (2 * d + 2 * GDN_QK + 2 * GDN_V) // Q_LORA
    ckv_blk = cq_blk + 1

    w_main, w_small = _layout_w_in(w_in, d)
    w_uq_b = _layout_w_uq(w_uq)
    w_ukv_b = w_ukv.astype(BF16)
    w_o_mla_b = w_o_mla.astype(BF16)
    w_o_gdn_b = w_o_gdn.astype(BF16)
    w_o_b = w_o.astype(BF16)
    w_gate_up_b = w_gate_up.astype(BF16)
    w_down_b = w_down.astype(BF16)

    for l in range(depth):
        sh_a, sc_a, gt_a, sh_f, sc_f, gt_f = [mod[l, :, i] for i in range(6)]
        p_main, p_small = _inproj(xs, norm_mix[l].reshape(1, d), sc_a, sh_a, w_main, w_small, l, seq)

        q, kv, kpe = _mla_proj(p_main, p_small, cos_t, sin_t, q_a_norm[l].reshape(1, -1),
                               kv_a_norm[l].reshape(1, -1), w_uq_b, w_ukv_b, l, cq_blk, ckv_blk)
        o_a = _flash(q.reshape(batch, seq, -1), kv.reshape(batch, seq, -1), kpe.reshape(batch, seq, -1))
        o_a = o_a.reshape(n, -1)

        qn, kn, vn, kt = _conv(p_main, conv_w[l], batch, seq, qkv_blk0)
        gate_flat = _gates(p_small, A_log[l], dt_bias[l])
        gate_rows = gate_flat.reshape(4 * h, batch, seq // CHUNK, CHUNK)
        u, w, qe, kdt, attn = _intra(qn, kn, vn, kt, gate_rows, gate_flat, batch, seq)
        o_b = _scan(u, w, qe, kdt, attn, gate_rows, p_main, gdn_norm[l].reshape(1, -1), batch, seq, z_blk)

        merged = _merge(o_a, o_b, w_o_mla_b, w_o_gdn_b, l, p_main, d)
        xs = _resid_matmul(merged, w_o_b, l, xs, gt_a, seq, 1024, 1024)

        act = _ffn_up(xs, norm_ffn[l].reshape(1, d), sc_f, sh_f, w_gate_up_b, l, seq)
        xs = _resid_matmul(act, w_down_b, l, xs, gt_f, seq, 1024, 512)

    return _final_norm(xs, final_norm.reshape(1, d)).reshape(batch, seq, d)
```

```python
import functools
import math

import jax
import jax.numpy as jnp
from jax import lax
from jax.experimental import pallas as pl
from jax.experimental.pallas import tpu as pltpu

F32 = jnp.float32
BF16 = jnp.bfloat16

MLA_HEADS = 8
QK_NOPE = 128
QK_ROPE = 64
V_HEAD = 128
Q_LORA = 512
KV_LORA = 512
ROPE_THETA = 10000.0
GDN_HEADS = 8
GDN_DK = 128
GDN_DV = 128
CONV_WIDTH = 4
CHUNK = 64
EPS = 1e-6

LANES = 128
MLA_QK_PAD = 2 * LANES
VMEM_LIMIT_BYTES = 56 * 1024 * 1024

GDN_QK = GDN_HEADS * GDN_DK
GDN_V = GDN_HEADS * GDN_DV


def _params(*semantics):
    return pltpu.CompilerParams(dimension_semantics=semantics, vmem_limit_bytes=VMEM_LIMIT_BYTES)


def _sigmoid(x):
    return 1.0 / (1.0 + jnp.exp(-x))


def _silu(x):
    return x * _sigmoid(x)


def _rms(x, w):
    return x * lax.rsqrt(jnp.mean(x * x, axis=-1, keepdims=True) + EPS) * w


def _bdot(a, b):
    return jnp.dot(a.astype(BF16), b.astype(BF16), preferred_element_type=F32)


def _ada_body(c_ref, w_ref, b_ref, o_ref):
    c = c_ref[...]
    o_ref[...] = _bdot(_silu(c), w_ref[...]) + b_ref[...]


def _ada(c, w_ada, b_ada):
    depth, d, n6 = w_ada.shape
    b = c.shape[0]
    tn = 1024
    return pl.pallas_call(
        _ada_body,
        out_shape=jax.ShapeDtypeStruct((depth, b, n6), F32),
        grid=(depth, n6 // tn),
        in_specs=[pl.BlockSpec((b, d), lambda l, j: (0, 0)),
                  pl.BlockSpec((None, d, tn), lambda l, j: (l, 0, j)),
                  pl.BlockSpec((None, 1, tn), lambda l, j: (l, 0, j))],
        out_specs=pl.BlockSpec((None, b, tn), lambda l, j: (l, 0, j)),
        compiler_params=_params("arbitrary", "arbitrary"),
        name="ada_mod",
    )(c, w_ada, b_ada.reshape(depth, 1, n6))


NORM_SLICES = 8


def _norm_slice(x_ref, nw_ref, sc_ref, sh_ref, h_ref):
    j = pl.program_id(1)
    rs = x_ref.shape[0] // NORM_SLICES
    r0 = pl.multiple_of(jnp.minimum(j, NORM_SLICES - 1) * rs, rs)
    h = _rms(x_ref[pl.ds(r0, rs), :], nw_ref[...]) * (1.0 + sc_ref[...]) + sh_ref[...]
    h_ref[pl.ds(r0, rs), :] = h.astype(BF16)


def _shifted_dispatch(n_tiles, h_even, h_odd, norm, matmul):
    i = pl.program_id(0)
    bufs = ((h_even, h_odd), (h_odd, h_even))

    @pl.when(i == 0)
    def _():
        norm(h_even)

    for parity, (fill, cur) in enumerate(bufs):
        @pl.when((i % 2 == parity) & (i > 0) & (i < n_tiles))
        def _(fill=fill, cur=cur):
            norm(fill)
            matmul(cur)

    @pl.when(i == n_tiles)
    def _():
        matmul(bufs[n_tiles % 2][1])


def _shifted_specs(tm, d, per_b, n_tiles):
    def xrow(i):
        return jnp.minimum(i, n_tiles - 1)

    in_specs = [pl.BlockSpec((tm, d), lambda i, j: (xrow(i), 0)),
                pl.BlockSpec((1, d), lambda i, j: (0, 0)),
                pl.BlockSpec((None, 1, d), lambda i, j: (xrow(i) // per_b, 0, 0)),
                pl.BlockSpec((None, 1, d), lambda i, j: (xrow(i) // per_b, 0, 0))]

    def orow(i):
        return jnp.maximum(i - 1, 0)

    def col(i, j):
        return jnp.where(i > 0, j, 0)

    return in_specs, orow, col


def _inproj_body(x_ref, nw_ref, sc_ref, sh_ref, w_ref, ws_ref, o_ref, os_ref, h_even, h_odd, *, n_tiles):
    def norm(fill):
        _norm_slice(x_ref, nw_ref, sc_ref, sh_ref, fill)

    def matmul(cur):
        o_ref[...] = jnp.dot(cur[...], w_ref[...], preferred_element_type=F32).astype(o_ref.dtype)

        @pl.when(pl.program_id(1) == 0)
        def _():
            os_ref[...] = jnp.dot(cur[...], ws_ref[...], preferred_element_type=F32)

    _shifted_dispatch(n_tiles, h_even, h_odd, norm, matmul)


def _inproj(x, nw, sc, sh, w_main, w_small, layer, seq):
    n, d = x.shape
    nm = w_main.shape[-1]
    ns = w_small.shape[-1]
    tm, tn = min(1024, seq), 1024
    n_tiles = n // tm
    in_specs, orow, col = _shifted_specs(tm, d, seq // tm, n_tiles)
    return pl.pallas_call(
        functools.partial(_inproj_body, n_tiles=n_tiles),
        out_shape=(jax.ShapeDtypeStruct((n, nm), BF16), jax.ShapeDtypeStruct((n, ns), F32)),
        grid=(n_tiles + 1, nm // tn),
        in_specs=in_specs + [pl.BlockSpec((None, d, tn), lambda i, j: (layer, 0, col(i, j))),
                             pl.BlockSpec((None, d, ns), lambda i, j: (layer, 0, 0))],
        out_specs=(pl.BlockSpec((tm, tn), lambda i, j: (orow(i), col(i, j))),
                   pl.BlockSpec((tm, ns), lambda i, j: (orow(i), 0))),
        scratch_shapes=[pltpu.VMEM((tm, d), BF16), pltpu.VMEM((tm, d), BF16)],
        compiler_params=_params("arbitrary", "arbitrary"),
        name="in_proj",
    )(x, nw, sc, sh, w_main, w_small)


def _rope128(x, cos, sin):
    return x * cos + pltpu.roll(x, LANES // 2, 1) * sin


def _mla_proj_body(cq_ref, ckv_ref, kpe_ref, cos_ref, sin_ref, qn_ref, kvn_ref, wq_ref, wkv_ref,
                   q_ref, kv_ref, kpeo_ref):
    cos = cos_ref[...]
    sin = sin_ref[...]
    scale = (QK_NOPE + QK_ROPE) ** -0.5 * math.log2(math.e)
    q = _bdot(_rms(cq_ref[...].astype(F32), qn_ref[...]), wq_ref[...])
    for h in range(MLA_HEADS):
        lo = h * MLA_QK_PAD
        q_ref[:, lo:lo + LANES] = (q[:, lo:lo + LANES] * scale).astype(BF16)
        pe = _rope128(q[:, lo + LANES:lo + 2 * LANES], cos, sin)
        q_ref[:, lo + LANES:lo + 2 * LANES] = (pe * scale).astype(BF16)
    kv_ref[...] = _bdot(_rms(ckv_ref[...].astype(F32), kvn_ref[...]), wkv_ref[...]).astype(BF16)
    kpeo_ref[...] = _rope128(kpe_ref[...], cos, sin).astype(BF16)


def _mla_proj(p_main, p_small, cos_t, sin_t, qn, kvn, wq, wkv, layer, cq_blk, ckv_blk):
    n = p_main.shape[0]
    tm = 512
    nq, nkv = wq.shape[-1], wkv.shape[-1]
    return pl.pallas_call(
        _mla_proj_body,
        out_shape=(jax.ShapeDtypeStruct((n, nq), BF16), jax.ShapeDtypeStruct((n, nkv), BF16),
                   jax.ShapeDtypeStruct((n, LANES), BF16)),
        grid=(n // tm,),
        in_specs=[pl.BlockSpec((tm, Q_LORA), lambda i: (i, cq_blk)),
                  pl.BlockSpec((tm, KV_LORA), lambda i: (i, ckv_blk)),
                  pl.BlockSpec((tm, LANES), lambda i: (i, 0)),
                  pl.BlockSpec((tm, LANES), lambda i: (i, 0)),
                  pl.BlockSpec((tm, LANES), lambda i: (i, 0)),
                  pl.BlockSpec((1, Q_LORA), lambda i: (0, 0)),
                  pl.BlockSpec((1, KV_LORA), lambda i: (0, 0)),
                  pl.BlockSpec((None, Q_LORA, nq), lambda i: (layer, 0, 0)),
                  pl.BlockSpec((None, KV_LORA, nkv), lambda i: (layer, 0, 0))],
        out_specs=(pl.BlockSpec((tm, nq), lambda i: (i, 0)),
                   pl.BlockSpec((tm, nkv), lambda i: (i, 0)),
                   pl.BlockSpec((tm, LANES), lambda i: (i, 0))),
        compiler_params=_params("arbitrary"),
        name="mla_proj",
    )(p_main, p_main, p_small, cos_t, sin_t, qn, kvn, wq, wkv)


def _flash_body(q_ref, kv_ref, kpe_ref, o_ref, *, tq, nq):
    tiles = [slice(i * tq, (i + 1) * tq) for i in range(nq)]
    row = lax.broadcasted_iota(jnp.int32, (tq, tq), 0)
    col = lax.broadcasted_iota(jnp.int32, (tq, tq), 1)
    causal = row >= col

    def scores(r):
        k = jnp.concatenate([kv_ref[tiles[r], 0:QK_NOPE], kpe_ref[tiles[r], :]], axis=1)
        return [lax.dot_general(q_ref[tiles[i], :], k, (((1,), (1,)), ((), ())), preferred_element_type=F32)
                for i in range(r, nq)]

    m = [None] * nq
    l = [None] * nq
    acc = [None] * nq
    s_next = scores(0)
    for r in range(nq):
        s_cur = s_next
        if r + 1 < nq:
            s_next = scores(r + 1)
        v = kv_ref[tiles[r], QK_NOPE:QK_NOPE + V_HEAD]
        p = []
        alpha = []
        for i, s in zip(range(r, nq), s_cur):
            if i == r:
                s = jnp.where(causal, s, -jnp.inf)
            s_max = jnp.max(s, axis=-1, keepdims=True)
            m_new = s_max if r == 0 else jnp.maximum(m[i], s_max)
            e = jnp.exp2(s - m_new)
            e_sum = jnp.sum(e, axis=-1, keepdims=True)
            if r == 0:
                alpha.append(None)
                l[i] = e_sum
            else:
                a = jnp.exp2(m[i] - m_new)
                alpha.append(a)
                l[i] = a * l[i] + e_sum
            m[i] = m_new
            p.append(e.astype(BF16))
        for i, pi, a in zip(range(r, nq), p, alpha):
            pv = jnp.dot(pi, v, preferred_element_type=F32)
            acc[i] = pv if a is None else a * acc[i] + pv
        o_ref[tiles[r], :] = (acc[r] / l[r]).astype(o_ref.dtype)


def _flash(q, kv, kpe):
    b, t, _ = q.shape
    tq = min(512, t)
    return pl.pallas_call(
        functools.partial(_flash_body, tq=tq, nq=t // tq),
        out_shape=jax.ShapeDtypeStruct((b, t, MLA_HEADS * V_HEAD), BF16),
        grid=(b, MLA_HEADS),
        in_specs=[pl.BlockSpec((None, t, MLA_QK_PAD), lambda bi, h: (bi, 0, h)),
                  pl.BlockSpec((None, t, QK_NOPE + V_HEAD), lambda bi, h: (bi, 0, h)),
                  pl.BlockSpec((None, t, LANES), lambda bi, h: (bi, 0, 0))],
        out_specs=pl.BlockSpec((None, t, V_HEAD), lambda bi, h: (bi, 0, h)),
        compiler_params=_params("arbitrary", "arbitrary"),
        name="mla_flash",
    )(q, kv, kpe)


def _conv_silu(x_ref, w_ref):
    u = x_ref[...].astype(F32)
    w = w_ref[...]
    sub = 8
    row = lax.broadcasted_iota(jnp.int32, (sub, u.shape[1]), 0)
    y = u * w[CONV_WIDTH - 1:CONV_WIDTH, :]
    for s in range(1, CONV_WIDTH):
        rolled = pltpu.roll(u, s, 0)
        shifted = jnp.concatenate([jnp.where(row >= s, rolled[0:sub], 0.0), rolled[sub:]], axis=0)
        y = y + shifted * w[CONV_WIDTH - 1 - s:CONV_WIDTH - s, :]
    return _silu(y)


def _l2norm(y):
    return y * lax.rsqrt(jnp.sum(y * y, axis=-1, keepdims=True) + EPS)


def _conv_body(xq_ref, xk_ref, xv_ref, wq_ref, wk_ref, wv_ref, q_ref, k_ref, v_ref, kt_ref):
    q_ref[...] = (_l2norm(_conv_silu(xq_ref, wq_ref)) * GDN_DK ** -0.5).astype(BF16)
    k = _l2norm(_conv_silu(xk_ref, wk_ref))
    k_ref[...] = k.astype(BF16)
    kt_ref[...] = k.T.astype(BF16)
    v_ref[...] = _conv_silu(xv_ref, wv_ref).astype(BF16)


def _conv(p_main, conv_w, batch, seq, col_blk0):
    n = p_main.shape[0]
    h = GDN_HEADS
    wide = jax.ShapeDtypeStruct((n, h * LANES), BF16)
    out_spec = pl.BlockSpec((seq, LANES), lambda b, hh: (b, hh))

    def x_spec(kind):
        return pl.BlockSpec((seq, LANES), lambda b, hh: (b, col_blk0 + kind * h + hh))

    def w_spec(kind):
        return pl.BlockSpec((CONV_WIDTH, LANES), lambda b, hh: (0, kind * h + hh))

    return pl.pallas_call(
        _conv_body,
        out_shape=(wide, wide, wide, jax.ShapeDtypeStruct((batch, h, LANES, seq), BF16)),
        grid=(batch, h),
        in_specs=[x_spec(0), x_spec(1), x_spec(2), w_spec(0), w_spec(1), w_spec(2)],
        out_specs=(out_spec, out_spec, out_spec,
                   pl.BlockSpec((None, None, LANES, seq), lambda b, hh: (b, hh, 0, 0))),
        compiler_params=_params("arbitrary", "arbitrary"),
        name="gdn_conv",
    )(p_main, p_main, p_main, conv_w, conv_w, conv_w)


def _split3(x):
    hi = x.astype(BF16).astype(F32)
    mid = (x - hi).astype(BF16).astype(F32)
    lo = x - hi - mid
    return hi, mid, lo


def _gate_body(s_ref, alog_ref, dtb_ref, o_ref):
    tm = s_ref.shape[0]
    h = GDN_HEADS
    t = s_ref[...].T
    beta = _sigmoid(t[0:h])
    a = t[h:2 * h] + dtb_ref[...]
    softplus = jnp.maximum(a, 0.0) + jnp.log(1.0 + jnp.exp(-jnp.abs(a)))
    g = -jnp.exp(alog_ref[...]) * softplus
    r = lax.broadcasted_iota(jnp.int32, (tm, tm), 0)
    c = lax.broadcasted_iota(jnp.int32, (tm, tm), 1)
    shift = CHUNK.bit_length() - 1
    same = lax.shift_right_logical(r, shift) == lax.shift_right_logical(c, shift)
    cum_m = jnp.where(same & (r <= c), 1.0, 0.0).astype(BF16)
    tot_m = jnp.where(same, 1.0, 0.0).astype(BF16)
    parts = jnp.concatenate(_split3(g), axis=0).astype(BF16)
    cum3 = jnp.dot(parts, cum_m, preferred_element_type=F32)
    tot3 = jnp.dot(parts, tot_m, preferred_element_type=F32)
    gc = cum3[0:h] + cum3[h:2 * h] + cum3[2 * h:3 * h]
    gl = tot3[0:h] + tot3[h:2 * h] + tot3[2 * h:3 * h]
    o_ref[0:h, :] = beta
    o_ref[h:2 * h, :] = gc
    o_ref[2 * h:3 * h, :] = jnp.exp(gl - gc)
    o_ref[3 * h:4 * h, :] = jnp.exp(gl)


def _gates(p_small, a_log, dt_bias):
    n = p_small.shape[0]
    tm = 512
    h = GDN_HEADS
    return pl.pallas_call(
        _gate_body,
        out_shape=jax.ShapeDtypeStruct((4 * h, n), F32),
        grid=(n // tm,),
        in_specs=[pl.BlockSpec((tm, LANES), lambda i: (i, 1)),
                  pl.BlockSpec((h, 1), lambda i: (0, 0)),
                  pl.BlockSpec((h, 1), lambda i: (0, 0))],
        out_specs=pl.BlockSpec((4 * h, tm), lambda i: (0, i)),
        compiler_params=_params("arbitrary"),
        name="gdn_gates",
    )(p_small, a_log.reshape(h, 1), dt_bias.reshape(h, 1))


def _intra_body(q_ref, k_ref, v_ref, kt_ref, beta_ref, gc_ref, dec_ref,
                u_ref, w_ref, qe_ref, kdt_ref, attn_ref, *, nc):
    hh = pl.program_id(1)
    ri = lax.broadcasted_iota(jnp.int32, (CHUNK, CHUNK), 0)
    ci = lax.broadcasted_iota(jnp.int32, (CHUNK, CHUNK), 1)
    eye = ri == ci
    lower = ri >= ci
    strict = ri > ci
    kt = kt_ref[...]
    kdt_ref[...] = (kt.astype(F32) * dec_ref[pl.ds(hh, 1), :]).astype(BF16)

    def to_col(rowvec):
        return jnp.sum(jnp.where(eye, jnp.broadcast_to(rowvec, (CHUNK, CHUNK)), 0.0), axis=1, keepdims=True)

    cs = range(nc)
    rows = [slice(c * CHUNK, (c + 1) * CHUNK) for c in cs]
    g_row = [gc_ref[c:c + 1, :] for c in cs]
    g_col = [to_col(g) for g in g_row]
    b_col = [to_col(beta_ref[c:c + 1, :]) for c in cs]
    eg_col = [jnp.exp(g) for g in g_col]
    decay = [jnp.where(lower, jnp.exp(jnp.where(lower, gc - gr, 0.0)), 0.0) for gc, gr in zip(g_col, g_row)]
    ktc = [kt[:, r] for r in rows]
    qb = [q_ref[r, :] for r in rows]
    kb = [k_ref[r, :].astype(F32) * b for r, b in zip(rows, b_col)]
    rhs = [jnp.concatenate([v_ref[r, :].astype(F32) * b, k * e], axis=1)
           for r, b, k, e in zip(rows, b_col, kb, eg_col)]
    kk = [_bdot(k, t) for k, t in zip(kb, ktc)]
    qk = [_bdot(q, t) for q, t in zip(qb, ktc)]
    lmat = [jnp.where(strict, k * d, 0.0) for k, d in zip(kk, decay)]
    xt = None
    s = 1
    while s < CHUNK:
        shift = s.bit_length() - 1
        bi = lax.shift_right_logical(ri, shift)
        bj = lax.shift_right_logical(ci, shift)
        pair = (lax.shift_right_logical(bi, 1) == lax.shift_right_logical(bj, 1)) & ((bi & 1) == 1) & ((bj & 1) == 0)
        cblk = [jnp.where(pair, m, 0.0) for m in lmat]
        if xt is None:
            xt = [-c for c in cblk]
        else:
            y = [c + _bdot(x, c) for x, c in zip(xt, cblk)]
            xt = [x - (a + _bdot(a, x)) for x, a in zip(xt, y)]
        s *= 2
    sol = [r + _bdot(x, r) for x, r in zip(xt, rhs)]
    for c in cs:
        u_ref[rows[c], :] = sol[c][:, 0:GDN_DV].astype(BF16)
        w_ref[rows[c], :] = sol[c][:, GDN_DV:GDN_DV + GDN_DK].astype(BF16)
        qe_ref[rows[c], :] = (qb[c].astype(F32) * eg_col[c]).astype(BF16)
        attn_ref[rows[c], :] = jnp.where(lower, qk[c] * decay[c], 0.0).astype(BF16)


def _intra(qn, kn, vn, kt, gate_rows, gate_flat, batch, seq):
    n = qn.shape[0]
    h = GDN_HEADS
    tt = min(2048, seq)
    nc = tt // CHUNK
    per_b = seq // tt
    wide = jax.ShapeDtypeStruct((n, h * LANES), BF16)
    row_spec = pl.BlockSpec((tt, LANES), lambda b, hh, t: (b * per_b + t, hh))
    return pl.pallas_call(
        functools.partial(_intra_body, nc=nc),
        out_shape=(wide, wide, wide,
                   jax.ShapeDtypeStruct((batch, h, LANES, seq), BF16),
                   jax.ShapeDtypeStruct((batch, h, seq, CHUNK), BF16)),
        grid=(batch, h, per_b),
        in_specs=[row_spec, row_spec, row_spec,
                  pl.BlockSpec((None, None, LANES, tt), lambda b, hh, t: (b, hh, 0, t)),
                  pl.BlockSpec((None, None, nc, CHUNK), lambda b, hh, t: (hh, b, t, 0)),
                  pl.BlockSpec((None, None, nc, CHUNK), lambda b, hh, t: (h + hh, b, t, 0)),
                  pl.BlockSpec((h, tt), lambda b, hh, t: (2, b * per_b + t))],
        out_specs=(row_spec, row_spec, row_spec,
                   pl.BlockSpec((None, None, LANES, tt), lambda b, hh, t: (b, hh, 0, t)),
                   pl.BlockSpec((None, None, tt, CHUNK), lambda b, hh, t: (b, hh, t, 0))),
        compiler_params=_params("arbitrary", "arbitrary", "arbitrary"),
        name="gdn_intra",
    )(qn, kn, vn, kt, gate_rows, gate_rows, gate_flat)


def _scan_body(u_ref, w_ref, qe_ref, kdt_ref, attn_ref, egl_ref, z_ref, gn_ref, o_ref, s_scr, *, nct):
    t = pl.program_id(1)

    @pl.when(t == 0)
    def _():
        s_scr[...] = jnp.zeros(s_scr.shape, F32)

    gn = gn_ref[...]
    hs = range(GDN_HEADS)
    cols = [slice(h * LANES, (h + 1) * LANES) for h in hs]
    state = [s_scr[h] for h in hs]
    for c in range(nct):
        rows = slice(c * CHUNK, (c + 1) * CHUNK)
        wq = [jnp.concatenate([w_ref[rows, cl], qe_ref[rows, cl]], axis=0) for cl in cols]
        r = [jnp.dot(a, s.astype(BF16), preferred_element_type=F32) for a, s in zip(wq, state)]
        v_new = [(u_ref[rows, cl].astype(F32) - x[0:CHUNK]).astype(BF16) for cl, x in zip(cols, r)]
        eg = [egl_ref[h, pl.ds(t * nct + c, 1), :][:, 0:1] for h in hs]
        state = [s * e + jnp.dot(kdt_ref[h, :, c * CHUNK:(c + 1) * CHUNK], v, preferred_element_type=F32)
                 for h, s, e, v in zip(hs, state, eg, v_new)]
        o = [x[CHUNK:2 * CHUNK] + jnp.dot(attn_ref[h, rows, :], v, preferred_element_type=F32)
             for h, x, v in zip(hs, r, v_new)]
        for h in hs:
            on = o[h] * lax.rsqrt(jnp.mean(o[h] * o[h], axis=-1, keepdims=True) + EPS) * gn
            o_ref[rows, cols[h]] = (on * _silu(z_ref[rows, cols[h]].astype(F32))).astype(BF16)
    for h in hs:
        s_scr[h] = state[h]


def _scan(u, w, qe, kdt, attn, egl, p_main, gdn_norm, batch, seq, z_blk):
    n = u.shape[0]
    h = GDN_HEADS
    tt = min(256, seq)
    nct = tt // CHUNK
    per_b = seq // tt
    wide_spec = pl.BlockSpec((tt, h * LANES), lambda b, t: (b * per_b + t, 0))
    return pl.pallas_call(
        functools.partial(_scan_body, nct=nct),
        out_shape=jax.ShapeDtypeStruct((n, h * LANES), BF16),
        grid=(batch, per_b),
        in_specs=[wide_spec, wide_spec, wide_spec,
                  pl.BlockSpec((None, h, LANES, tt), lambda b, t: (b, 0, 0, t)),
                  pl.BlockSpec((None, h, tt, CHUNK), lambda b, t: (b, 0, t, 0)),
                  pl.BlockSpec((h, None, seq // CHUNK, CHUNK), lambda b, t: (3, b, 0, 0)),
                  pl.BlockSpec((tt, h * LANES), lambda b, t: (b * per_b + t, z_blk)),
                  pl.BlockSpec((1, LANES), lambda b, t: (0, 0))],
        out_specs=wide_spec,
        scratch_shapes=[pltpu.VMEM((h, GDN_DK, GDN_DV), F32)],
        compiler_params=_params("arbitrary", "arbitrary"),
        name="gdn_scan",
    )(u, w, qe, kdt, attn, egl, p_main, gdn_norm)


def _merge_body(a_ref, b_ref, wa_ref, wb_ref, ga_ref, gb_ref, o_ref):
    ya = jnp.dot(a_ref[...], wa_ref[...], preferred_element_type=F32)
    yb = jnp.dot(b_ref[...], wb_ref[...], preferred_element_type=F32)
    o = _sigmoid(ga_ref[...].astype(F32)) * ya + _sigmoid(gb_ref[...].astype(F32)) * yb
    o_ref[...] = o.astype(o_ref.dtype)


def _merge(o_a, o_b, w_a, w_b, layer, p_main, d):
    n, ka = o_a.shape
    kb = o_b.shape[1]
    tm, tn = 1024, 1024
    nb = d // tn
    return pl.pallas_call(
        _merge_body,
        out_shape=jax.ShapeDtypeStruct((n, d), BF16),
        grid=(n // tm, nb),
        in_specs=[pl.BlockSpec((tm, ka), lambda i, j: (i, 0)),
                  pl.BlockSpec((tm, kb), lambda i, j: (i, 0)),
                  pl.BlockSpec((None, ka, tn), lambda i, j: (layer, 0, j)),
                  pl.BlockSpec((None, kb, tn), lambda i, j: (layer, 0, j)),
                  pl.BlockSpec((tm, tn), lambda i, j: (i, j)),
                  pl.BlockSpec((tm, tn), lambda i, j: (i, nb + j))],
        out_specs=pl.BlockSpec((tm, tn), lambda i, j: (i, j)),
        compiler_params=_params("arbitrary", "arbitrary"),
        name="branch_merge",
    )(o_a, o_b, w_a, w_b, p_main, p_main)


def _resid_body(a_ref, w_ref, x_ref, gt_ref, o_ref):
    y = jnp.dot(a_ref[...], w_ref[...], preferred_element_type=F32)
    o_ref[...] = x_ref[...] + gt_ref[...] * y


def _resid_matmul(a, w, layer, x, gt, seq, tm, tn):
    n, k = a.shape
    d = w.shape[-1]
    tm = min(tm, seq)
    per_b = seq // tm
    return pl.pallas_call(
        _resid_body,
        out_shape=jax.ShapeDtypeStruct((n, d), F32),
        grid=(n // tm, d // tn),
        in_specs=[pl.BlockSpec((tm, k), lambda i, j: (i, 0)),
                  pl.BlockSpec((None, k, tn), lambda i, j: (layer, 0, j)),
                  pl.BlockSpec((tm, tn), lambda i, j: (i, j)),
                  pl.BlockSpec((None, 1, tn), lambda i, j: (i // per_b, 0, j))],
        out_specs=pl.BlockSpec((tm, tn), lambda i, j: (i, j)),
        compiler_params=_params("arbitrary", "arbitrary"),
        name="resid_matmul",
    )(a, w, x, gt)


def _ffn_up_body(x_ref, nw_ref, sc_ref, sh_ref, wg_ref, wu_ref, o_ref, h_even, h_odd, *, n_tiles):
    def norm(fill):
        _norm_slice(x_ref, nw_ref, sc_ref, sh_ref, fill)

    def matmul(cur):
        hb = cur[...]
        gate = jnp.dot(hb, wg_ref[...], preferred_element_type=F32)
        up = jnp.dot(hb, wu_ref[...], preferred_element_type=F32)
        o_ref[...] = (_silu(gate) * up).astype(o_ref.dtype)

    _shifted_dispatch(n_tiles, h_even, h_odd, norm, matmul)


def _ffn_up(x, nw, sc, sh, w_gu, layer, seq):
    n, d = x.shape
    dff = w_gu.shape[-1] // 2
    tm, tn = min(1024, seq), 512
    n_tiles = n // tm
    nb = dff // tn
    in_specs, orow, col = _shifted_specs(tm, d, seq // tm, n_tiles)
    return pl.pallas_call(
        functools.partial(_ffn_up_body, n_tiles=n_tiles),
        out_shape=jax.ShapeDtypeStruct((n, dff), BF16),
        grid=(n_tiles + 1, nb),
        in_specs=in_specs + [pl.BlockSpec((None, d, tn), lambda i, j: (layer, 0, col(i, j))),
                             pl.BlockSpec((None, d, tn), lambda i, j: (layer, 0, nb + col(i, j)))],
        out_specs=pl.BlockSpec((tm, tn), lambda i, j: (orow(i), col(i, j))),
        scratch_shapes=[pltpu.VMEM((tm, d), BF16), pltpu.VMEM((tm, d), BF16)],
        compiler_params=_params("arbitrary", "arbitrary"),
        name="ffn_up",
    )(x, nw, sc, sh, w_gu, w_gu)


def _final_norm_body(x_ref, w_ref, o_ref):
    o_ref[...] = _rms(x_ref[...], w_ref[...])


def _final_norm(x, w):
    n, d = x.shape
    tm = 512
    return pl.pallas_call(
        _final_norm_body,
        out_shape=jax.ShapeDtypeStruct((n, d), F32),
        grid=(n // tm,),
        in_specs=[pl.BlockSpec((tm, d), lambda i: (i, 0)), pl.BlockSpec((1, d), lambda i: (0, 0))],
        out_specs=pl.BlockSpec((tm, d), lambda i: (i, 0)),
        compiler_params=_params("arbitrary"),
        name="final_norm",
    )(x, w)


def _pad_rope_cols(w):
    half = QK_ROPE // 2
    z = jnp.zeros(w.shape[:-1] + (LANES // 2 - half,), w.dtype)
    return jnp.concatenate([w[..., :half], z, w[..., half:], z], axis=-1)


def _regroup_body(a_ref, b_ref, o_ref, *, regions, tn):
    t = pl.program_id(1)
    first = 0
    for src, n_cols in regions:
        tiles = n_cols // tn
        shift = src % LANES

        @pl.when((t >= first) & (t < first + tiles))
        def _(shift=shift):
            a = a_ref[...]
            if shift:
                a = jnp.concatenate([a[:, shift:], b_ref[:, :shift]], axis=1)
            o_ref[...] = a.astype(BF16)

        first += tiles


def _regroup_cols(w, regions, tn=512):
    depth, k, _ = w.shape
    total = sum(n for _, n in regions)

    def block_ids(t):
        a_id, b_id, first = 0, 0, 0
        for src, n_cols in regions:
            base = src - src % LANES
            assert base % tn == 0 and n_cols % tn == 0
            inside = t >= first
            a_id = jnp.where(inside, base // tn + t - first, a_id)
            b_id = jnp.where(inside, (base + tn) // LANES + (t - first) * (tn // LANES), b_id)
            first += n_cols // tn
        return a_id, b_id

    return pl.pallas_call(
        functools.partial(_regroup_body, regions=tuple(regions), tn=tn),
        out_shape=jax.ShapeDtypeStruct((depth, k, total), BF16),
        grid=(depth, total // tn),
        in_specs=[pl.BlockSpec((None, k, tn), lambda l, t: (l, 0, block_ids(t)[0])),
                  pl.BlockSpec((None, k, LANES), lambda l, t: (l, 0, block_ids(t)[1]))],
        out_specs=pl.BlockSpec((None, k, tn), lambda l, t: (l, 0, t)),
        compiler_params=_params("arbitrary", "arbitrary"),
        name="regroup_cols",
    )(w, w)


def _layout_w_in(w_in, d):
    o_kpe = Q_LORA + KV_LORA
    o_qkvz = o_kpe + QK_ROPE
    o_ba = o_qkvz + 2 * GDN_QK + 2 * GDN_V
    o_gates = o_ba + 2 * GDN_HEADS
    w_main = _regroup_cols(w_in, [(o_gates, 2 * d), (o_qkvz, 2 * GDN_QK + 2 * GDN_V), (0, Q_LORA + KV_LORA)])
    kpe = w_in[..., o_kpe:o_kpe + QK_ROPE].astype(BF16)
    ba = w_in[..., o_ba:o_ba + 2 * GDN_HEADS].astype(BF16)
    pad = jnp.zeros(w_in.shape[:-1] + (LANES - 2 * GDN_HEADS,), BF16)
    w_small = jnp.concatenate([_pad_rope_cols(kpe), ba, pad], axis=-1)
    return w_main, w_small


def _layout_w_uq(w_uq):
    lead = w_uq.shape[:-1]
    w = w_uq.astype(BF16).reshape(lead + (MLA_HEADS, QK_NOPE + QK_ROPE))
    w = jnp.concatenate([w[..., :QK_NOPE], _pad_rope_cols(w[..., QK_NOPE:])], axis=-1)
    return w.reshape(lead + (MLA_HEADS * MLA_QK_PAD,))


def _rope_tables(positions):
    inv_freq = 1.0 / (ROPE_THETA ** (jnp.arange(0, QK_ROPE, 2, dtype=F32) / QK_ROPE))
    ang = positions.astype(F32).reshape(-1)[:, None] * inv_freq
    cos, sin = jnp.cos(ang), jnp.sin(ang)
    z = jnp.zeros_like(cos)
    return jnp.concatenate([cos, z, cos, z], axis=1), jnp.concatenate([-sin, z, sin, z], axis=1)


def kernel(x, c, positions, w_ada, b_ada, norm_mix, norm_ffn, w_in, q_a_norm, kv_a_norm, w_uq, w_ukv, w_o_mla,
           conv_w, A_log, dt_bias, gdn_norm, w_o_gdn, w_o, w_gate_up, w_down, final_norm):
    batch, seq, d = x.shape
    depth = w_ada.shape[0]
    n = batch * seq
    h = GDN_HEADS
    cos_t, sin_t = _rope_tables(positions)
    mod = _ada(c, w_ada, b_ada).reshape(depth, batch, 6, 1, d)
    xs = x.reshape(n, d)

    qkv_blk0 = 2 * d // LANES
    z_blk = (2 * d + 2 * GDN_QK + GDN_V) // (h * LANES)
    cq_blk = (2 * d + 2 * GDN_QK + 2 * GDN_V) // Q_LORA
    ckv_blk = cq_blk + 1

    w_main, w_small = _layout_w_in(w_in, d)
    w_uq_b = _layout_w_uq(w_uq)
    w_ukv_b = w_ukv.astype(BF16)
    w_o_mla_b = w_o_mla.astype(BF16)
    w_o_gdn_b = w_o_gdn.astype(BF16)
    w_o_b = w_o.astype(BF16)
    w_gate_up_b = w_gate_up.astype(BF16)
    w_down_b = w_down.astype(BF16)

    for l in range(depth):
        sh_a, sc_a, gt_a, sh_f, sc_f, gt_f = [mod[l, :, i] for i in range(6)]
        p_main, p_small = _inproj(xs, norm_mix[l].reshape(1, d), sc_a, sh_a, w_main, w_small, l, seq)

        q, kv, kpe = _mla_proj(p_main, p_small, cos_t, sin_t, q_a_norm[l].reshape(1, -1),
                               kv_a_norm[l].reshape(1, -1), w_uq_b, w_ukv_b, l, cq_blk, ckv_blk)
        o_a = _flash(q.reshape(batch, seq, -1), kv.reshape(batch, seq, -1), kpe.reshape(batch, seq, -1))
        o_a = o_a.reshape(n, -1)

        qn, kn, vn, kt = _conv(p_main, conv_w[l], batch, seq, qkv_blk0)
        gate_flat = _gates(p_small, A_log[l], dt_bias[l])
        gate_rows = gate_flat.reshape(4 * h, batch, seq // CHUNK, CHUNK)
        u, w, qe, kdt, attn = _intra(qn, kn, vn, kt, gate_rows, gate_flat, batch, seq)
        o_b = _scan(u, w, qe, kdt, attn, gate_rows, p_main, gdn_norm[l].reshape(1, -1), batch, seq, z_blk)

        merged = _merge(o_a, o_b, w_o_mla_b, w_o_gdn_b, l, p_main, d)
        xs = _resid_matmul(merged, w_o_b, l, xs, gt_a, seq, 1024, 1024)

        act = _ffn_up(xs, norm_ffn[l].reshape(1, d), sc_f, sh_f, w_gate_up_b, l, seq)
        xs = _resid_matmul(act, w_down_b, l, xs, gt_f, seq, 1024, 512)

    return _final_norm(xs, final_norm.reshape(1, d)).reshape(batch, seq, d)
```

```python
import functools
import math

import jax
import jax.numpy as jnp
from jax import lax
from jax.experimental import pallas as pl
from jax.experimental.pallas import tpu as pltpu

F32 = jnp.float32
BF16 = jnp.bfloat16

MLA_HEADS = 8
QK_NOPE = 128
QK_ROPE = 64
V_HEAD = 128
Q_LORA = 512
KV_LORA = 512
ROPE_THETA = 10000.0
GDN_HEADS = 8
GDN_DK = 128
GDN_DV = 128
CONV_WIDTH = 4
CHUNK = 64
EPS = 1e-6

LANES = 128
MLA_QK_PAD = 2 * LANES
VMEM_LIMIT_BYTES = 56 * 1024 * 1024

GDN_QK = GDN_HEADS * GDN_DK
GDN_V = GDN_HEADS * GDN_DV


def _params(*semantics):
    return pltpu.CompilerParams(dimension_semantics=semantics, vmem_limit_bytes=VMEM_LIMIT_BYTES)


def _sigmoid(x):
    return 1.0 / (1.0 + jnp.exp(-x))


def _silu(x):
    return x * _sigmoid(x)


def _rms(x, w):
    return x * lax.rsqrt(jnp.mean(x * x, axis=-1, keepdims=True) + EPS) * w


def _bdot(a, b):
    return jnp.dot(a.astype(BF16), b.astype(BF16), preferred_element_type=F32)


def _ada_body(c_ref, w_ref, b_ref, o_ref):
    c = c_ref[...]
    o_ref[...] = _bdot(_silu(c), w_ref[...]) + b_ref[...]


def _ada(c, w_ada, b_ada):
    depth, d, n6 = w_ada.shape
    b = c.shape[0]
    tn = 1024
    return pl.pallas_call(
        _ada_body,
        out_shape=jax.ShapeDtypeStruct((depth, b, n6), F32),
        grid=(depth, n6 // tn),
        in_specs=[pl.BlockSpec((b, d), lambda l, j: (0, 0)),
                  pl.BlockSpec((None, d, tn), lambda l, j: (l, 0, j)),
                  pl.BlockSpec((None, 1, tn), lambda l, j: (l, 0, j))],
        out_specs=pl.BlockSpec((None, b, tn), lambda l, j: (l, 0, j)),
        compiler_params=_params("arbitrary", "arbitrary"),
        name="ada_mod",
    )(c, w_ada, b_ada.reshape(depth, 1, n6))


NORM_SLICES = 8


def _norm_slice(x_ref, nw_ref, sc_ref, sh_ref, h_ref):
    j = pl.program_id(1)
    rs = x_ref.shape[0] // NORM_SLICES
    r0 = pl.multiple_of(jnp.minimum(j, NORM_SLICES - 1) * rs, rs)
    h = _rms(x_ref[pl.ds(r0, rs), :], nw_ref[...]) * (1.0 + sc_ref[...]) + sh_ref[...]
    h_ref[pl.ds(r0, rs), :] = h.astype(BF16)


def _shifted_dispatch(n_tiles, h_even, h_odd, norm, matmul):
    i = pl.program_id(0)
    bufs = ((h_even, h_odd), (h_odd, h_even))

    @pl.when(i == 0)
    def _():
        norm(h_even)

    for parity, (fill, cur) in enumerate(bufs):
        @pl.when((i % 2 == parity) & (i > 0) & (i < n_tiles))
        def _(fill=fill, cur=cur):
            norm(fill)
            matmul(cur)

    @pl.when(i == n_tiles)
    def _():
        matmul(bufs[n_tiles % 2][1])


def _shifted_specs(tm, d, per_b, n_tiles):
    def xrow(i):
        return jnp.minimum(i, n_tiles - 1)

    in_specs = [pl.BlockSpec((tm, d), lambda i, j: (xrow(i), 0)),
                pl.BlockSpec((1, d), lambda i, j: (0, 0)),
                pl.BlockSpec((None, 1, d), lambda i, j: (xrow(i) // per_b, 0, 0)),
                pl.BlockSpec((None, 1, d), lambda i, j: (xrow(i) // per_b, 0, 0))]

    def orow(i):
        return jnp.maximum(i - 1, 0)

    def col(i, j):
        return jnp.where(i > 0, j, 0)

    return in_specs, orow, col


def _inproj_body(x_ref, nw_ref, sc_ref, sh_ref, w_ref, ws_ref, o_ref, os_ref, h_even, h_odd, *, n_tiles):
    def norm(fill):
        _norm_slice(x_ref, nw_ref, sc_ref, sh_ref, fill)

    nt = (((1,), (1,)), ((), ()))

    def matmul(cur):
        o_ref[...] = lax.dot_general(cur[...], w_ref[...], nt, preferred_element_type=F32).astype(o_ref.dtype)

        @pl.when(pl.program_id(1) == 0)
        def _():
            os_ref[...] = lax.dot_general(cur[...], ws_ref[...], nt, preferred_element_type=F32)

    _shifted_dispatch(n_tiles, h_even, h_odd, norm, matmul)


def _inproj(x, nw, sc, sh, w_main, w_small, layer, seq):
    n, d = x.shape
    nm = w_main.shape[1]
    ns = w_small.shape[1]
    tm, tn = min(1024, seq), 1024
    n_tiles = n // tm
    in_specs, orow, col = _shifted_specs(tm, d, seq // tm, n_tiles)
    return pl.pallas_call(
        functools.partial(_inproj_body, n_tiles=n_tiles),
        out_shape=(jax.ShapeDtypeStruct((n, nm), BF16), jax.ShapeDtypeStruct((n, ns), F32)),
        grid=(n_tiles + 1, nm // tn),
        in_specs=in_specs + [pl.BlockSpec((None, tn, d), lambda i, j: (layer, col(i, j), 0)),
                             pl.BlockSpec((None, ns, d), lambda i, j: (layer, 0, 0))],
        out_specs=(pl.BlockSpec((tm, tn), lambda i, j: (orow(i), col(i, j))),
                   pl.BlockSpec((tm, ns), lambda i, j: (orow(i), 0))),
        scratch_shapes=[pltpu.VMEM((tm, d), BF16), pltpu.VMEM((tm, d), BF16)],
        compiler_params=_params("arbitrary", "arbitrary"),
        name="in_proj",
    )(x, nw, sc, sh, w_main, w_small)


def _rope128(x, cos, sin):
    return x * cos + pltpu.roll(x, LANES // 2, 1) * sin


def _mla_proj_body(cq_ref, ckv_ref, kpe_ref, cos_ref, sin_ref, qn_ref, kvn_ref, wq_ref, wkv_ref,
                   q_ref, kv_ref, kpeo_ref):
    cos = cos_ref[...]
    sin = sin_ref[...]
    scale = (QK_NOPE + QK_ROPE) ** -0.5 * math.log2(math.e)
    cqn = _rms(cq_ref[...].astype(F32), qn_ref[...]).astype(BF16)
    q = lax.dot_general(cqn, wq_ref[...], (((1,), (1,)), ((), ())), preferred_element_type=F32)
    for h in range(MLA_HEADS):
        lo = h * MLA_QK_PAD
        q_ref[:, lo:lo + LANES] = (q[:, lo:lo + LANES] * scale).astype(BF16)
        pe = _rope128(q[:, lo + LANES:lo + 2 * LANES], cos, sin)
        q_ref[:, lo + LANES:lo + 2 * LANES] = (pe * scale).astype(BF16)
    kv_ref[...] = _bdot(_rms(ckv_ref[...].astype(F32), kvn_ref[...]), wkv_ref[...]).astype(BF16)
    kpeo_ref[...] = _rope128(kpe_ref[...], cos, sin).astype(BF16)


def _mla_proj(p_main, p_small, cos_t, sin_t, qn, kvn, wq, wkv, layer, cq_blk, ckv_blk):
    n = p_main.shape[0]
    tm = 512
    nq, nkv = wq.shape[1], wkv.shape[-1]
    return pl.pallas_call(
        _mla_proj_body,
        out_shape=(jax.ShapeDtypeStruct((n, nq), BF16), jax.ShapeDtypeStruct((n, nkv), BF16),
                   jax.ShapeDtypeStruct((n, LANES), BF16)),
        grid=(n // tm,),
        in_specs=[pl.BlockSpec((tm, Q_LORA), lambda i: (i, cq_blk)),
                  pl.BlockSpec((tm, KV_LORA), lambda i: (i, ckv_blk)),
                  pl.BlockSpec((tm, LANES), lambda i: (i, 0)),
                  pl.BlockSpec((tm, LANES), lambda i: (i, 0)),
                  pl.BlockSpec((tm, LANES), lambda i: (i, 0)),
                  pl.BlockSpec((1, Q_LORA), lambda i: (0, 0)),
                  pl.BlockSpec((1, KV_LORA), lambda i: (0, 0)),
                  pl.BlockSpec((None, nq, Q_LORA), lambda i: (layer, 0, 0)),
                  pl.BlockSpec((None, KV_LORA, nkv), lambda i: (layer, 0, 0))],
        out_specs=(pl.BlockSpec((tm, nq), lambda i: (i, 0)),
                   pl.BlockSpec((tm, nkv), lambda i: (i, 0)),
                   pl.BlockSpec((tm, LANES), lambda i: (i, 0))),
        compiler_params=_params("arbitrary"),
        name="mla_proj",
    )(p_main, p_main, p_small, cos_t, sin_t, qn, kvn, wq, wkv)


def _flash_body(q_ref, kv_ref, kpe_ref, o_ref, *, tq, nq):
    tiles = [slice(i * tq, (i + 1) * tq) for i in range(nq)]
    row = lax.broadcasted_iota(jnp.int32, (tq, tq), 0)
    col = lax.broadcasted_iota(jnp.int32, (tq, tq), 1)
    causal = row >= col

    def scores(r):
        k = jnp.concatenate([kv_ref[tiles[r], 0:QK_NOPE], kpe_ref[tiles[r], :]], axis=1)
        return [lax.dot_general(q_ref[tiles[i], :], k, (((1,), (1,)), ((), ())), preferred_element_type=F32)
                for i in range(r, nq)]

    m = [None] * nq
    l = [None] * nq
    acc = [None] * nq
    s_next = scores(0)
    for r in range(nq):
        s_cur = s_next
        if r + 1 < nq:
            s_next = scores(r + 1)
        v = kv_ref[tiles[r], QK_NOPE:QK_NOPE + V_HEAD]
        p = []
        alpha = []
        for i, s in zip(range(r, nq), s_cur):
            if i == r:
                s = jnp.where(causal, s, -jnp.inf)
            s_max = jnp.max(s, axis=-1, keepdims=True)
            m_new = s_max if r == 0 else jnp.maximum(m[i], s_max)
            e = jnp.exp2(s - m_new)
            e_sum = jnp.sum(e, axis=-1, keepdims=True)
            if r == 0:
                alpha.append(None)
                l[i] = e_sum
            else:
                a = jnp.exp2(m[i] - m_new)
                alpha.append(a)
                l[i] = a * l[i] + e_sum
            m[i] = m_new
            p.append(e.astype(BF16))
        for i, pi, a in zip(range(r, nq), p, alpha):
            pv = jnp.dot(pi, v, preferred_element_type=F32)
            acc[i] = pv if a is None else a * acc[i] + pv
        o_ref[tiles[r], :] = (acc[r] / l[r]).astype(o_ref.dtype)


def _flash(q, kv, kpe):
    b, t, _ = q.shape
    tq = min(512, t)
    return pl.pallas_call(
        functools.partial(_flash_body, tq=tq, nq=t // tq),
        out_shape=jax.ShapeDtypeStruct((b, t, MLA_HEADS * V_HEAD), BF16),
        grid=(b, MLA_HEADS),
        in_specs=[pl.BlockSpec((None, t, MLA_QK_PAD), lambda bi, h: (bi, 0, h)),
                  pl.BlockSpec((None, t, QK_NOPE + V_HEAD), lambda bi, h: (bi, 0, h)),
                  pl.BlockSpec((None, t, LANES), lambda bi, h: (bi, 0, 0))],
        out_specs=pl.BlockSpec((None, t, V_HEAD), lambda bi, h: (bi, 0, h)),
        compiler_params=_params("arbitrary", "arbitrary"),
        name="mla_flash",
    )(q, kv, kpe)


def _conv_silu(x_ref, w_ref):
    u = x_ref[...].astype(F32)
    w = w_ref[...]
    sub = 8
    row = lax.broadcasted_iota(jnp.int32, (sub, u.shape[1]), 0)
    y = u * w[CONV_WIDTH - 1:CONV_WIDTH, :]
    for s in range(1, CONV_WIDTH):
        rolled = pltpu.roll(u, s, 0)
        shifted = jnp.concatenate([jnp.where(row >= s, rolled[0:sub], 0.0), rolled[sub:]], axis=0)
        y = y + shifted * w[CONV_WIDTH - 1 - s:CONV_WIDTH - s, :]
    return _silu(y)


def _l2norm(y):
    return y * lax.rsqrt(jnp.sum(y * y, axis=-1, keepdims=True) + EPS)


def _conv_body(xq_ref, xk_ref, xv_ref, wq_ref, wk_ref, wv_ref, q_ref, k_ref, v_ref, kt_ref):
    q_ref[...] = (_l2norm(_conv_silu(xq_ref, wq_ref)) * GDN_DK ** -0.5).astype(BF16)
    k = _l2norm(_conv_silu(xk_ref, wk_ref))
    k_ref[...] = k.astype(BF16)
    kt_ref[...] = k.T.astype(BF16)
    v_ref[...] = _conv_silu(xv_ref, wv_ref).astype(BF16)


def _conv(p_main, conv_w, batch, seq, col_blk0):
    n = p_main.shape[0]
    h = GDN_HEADS
    wide = jax.ShapeDtypeStruct((n, h * LANES), BF16)
    out_spec = pl.BlockSpec((seq, LANES), lambda b, hh: (b, hh))

    def x_spec(kind):
        return pl.BlockSpec((seq, LANES), lambda b, hh: (b, col_blk0 + kind * h + hh))

    def w_spec(kind):
        return pl.BlockSpec((CONV_WIDTH, LANES), lambda b, hh: (0, kind * h + hh))

    return pl.pallas_call(
        _conv_body,
        out_shape=(wide, wide, wide, jax.ShapeDtypeStruct((batch, h, LANES, seq), BF16)),
        grid=(batch, h),
        in_specs=[x_spec(0), x_spec(1), x_spec(2), w_spec(0), w_spec(1), w_spec(2)],
        out_specs=(out_spec, out_spec, out_spec,
                   pl.BlockSpec((None, None, LANES, seq), lambda b, hh: (b, hh, 0, 0))),
        compiler_params=_params("arbitrary", "arbitrary"),
        name="gdn_conv",
    )(p_main, p_main, p_main, conv_w, conv_w, conv_w)


def _split3(x):
    hi = x.astype(BF16).astype(F32)
    mid = (x - hi).astype(BF16).astype(F32)
    lo = x - hi - mid
    return hi, mid, lo


def _gate_body(s_ref, alog_ref, dtb_ref, o_ref):
    tm = s_ref.shape[0]
    h = GDN_HEADS
    t = s_ref[...].T
    beta = _sigmoid(t[0:h])
    a = t[h:2 * h] + dtb_ref[...]
    softplus = jnp.maximum(a, 0.0) + jnp.log(1.0 + jnp.exp(-jnp.abs(a)))
    g = -jnp.exp(alog_ref[...]) * softplus
    r = lax.broadcasted_iota(jnp.int32, (tm, tm), 0)
    c = lax.broadcasted_iota(jnp.int32, (tm, tm), 1)
    shift = CHUNK.bit_length() - 1
    same = lax.shift_right_logical(r, shift) == lax.shift_right_logical(c, shift)
    cum_m = jnp.where(same & (r <= c), 1.0, 0.0).astype(BF16)
    tot_m = jnp.where(same, 1.0, 0.0).astype(BF16)
    parts = jnp.concatenate(_split3(g), axis=0).astype(BF16)
    cum3 = jnp.dot(parts, cum_m, preferred_element_type=F32)
    tot3 = jnp.dot(parts, tot_m, preferred_element_type=F32)
    gc = cum3[0:h] + cum3[h:2 * h] + cum3[2 * h:3 * h]
    gl = tot3[0:h] + tot3[h:2 * h] + tot3[2 * h:3 * h]
    o_ref[0:h, :] = beta
    o_ref[h:2 * h, :] = gc
    o_ref[2 * h:3 * h, :] = jnp.exp(gl - gc)
    o_ref[3 * h:4 * h, :] = jnp.exp(gl)


def _gates(p_small, a_log, dt_bias):
    n = p_small.shape[0]
    tm = 512
    h = GDN_HEADS
    return pl.pallas_call(
        _gate_body,
        out_shape=jax.ShapeDtypeStruct((4 * h, n), F32),
        grid=(n // tm,),
        in_specs=[pl.BlockSpec((tm, LANES), lambda i: (i, 1)),
                  pl.BlockSpec((h, 1), lambda i: (0, 0)),
                  pl.BlockSpec((h, 1), lambda i: (0, 0))],
        out_specs=pl.BlockSpec((4 * h, tm), lambda i: (0, i)),
        compiler_params=_params("arbitrary"),
        name="gdn_gates",
    )(p_small, a_log.reshape(h, 1), dt_bias.reshape(h, 1))


def _intra_body(q_ref, k_ref, v_ref, kt_ref, beta_ref, gc_ref, dec_ref,
                u_ref, w_ref, qe_ref, kdt_ref, attn_ref, *, nc):
    hh = pl.program_id(1)
    ri = lax.broadcasted_iota(jnp.int32, (CHUNK, CHUNK), 0)
    ci = lax.broadcasted_iota(jnp.int32, (CHUNK, CHUNK), 1)
    eye = ri == ci
    lower = ri >= ci
    strict = ri > ci
    kt = kt_ref[...]
    kdt_ref[...] = (kt.astype(F32) * dec_ref[pl.ds(hh, 1), :]).astype(BF16)

    def to_col(rowvec):
        return jnp.sum(jnp.where(eye, jnp.broadcast_to(rowvec, (CHUNK, CHUNK)), 0.0), axis=1, keepdims=True)

    cs = range(nc)
    rows = [slice(c * CHUNK, (c + 1) * CHUNK) for c in cs]
    g_row = [gc_ref[c:c + 1, :] for c in cs]
    g_col = [to_col(g) for g in g_row]
    b_col = [to_col(beta_ref[c:c + 1, :]) for c in cs]
    eg_col = [jnp.exp(g) for g in g_col]
    decay = [jnp.where(lower, jnp.exp(jnp.where(lower, gc - gr, 0.0)), 0.0) for gc, gr in zip(g_col, g_row)]
    ktc = [kt[:, r] for r in rows]
    qb = [q_ref[r, :] for r in rows]
    kb = [k_ref[r, :].astype(F32) * b for r, b in zip(rows, b_col)]
    rhs = [jnp.concatenate([v_ref[r, :].astype(F32) * b, k * e], axis=1)
           for r, b, k, e in zip(rows, b_col, kb, eg_col)]
    kk = [_bdot(k, t) for k, t in zip(kb, ktc)]
    qk = [_bdot(q, t) for q, t in zip(qb, ktc)]
    lmat = [jnp.where(strict, k * d, 0.0) for k, d in zip(kk, decay)]
    xt = None
    s = 1
    while s < CHUNK:
        shift = s.bit_length() - 1
        bi = lax.shift_right_logical(ri, shift)
        bj = lax.shift_right_logical(ci, shift)
        pair = (lax.shift_right_logical(bi, 1) == lax.shift_right_logical(bj, 1)) & ((bi & 1) == 1) & ((bj & 1) == 0)
        cblk = [jnp.where(pair, m, 0.0) for m in lmat]
        if xt is None:
            xt = [-c for c in cblk]
        else:
            y = [c + _bdot(x, c) for x, c in zip(xt, cblk)]
            xt = [x - (a + _bdot(a, x)) for x, a in zip(xt, y)]
        s *= 2
    sol = [r + _bdot(x, r) for x, r in zip(xt, rhs)]
    for c in cs:
        u_ref[rows[c], :] = sol[c][:, 0:GDN_DV].astype(BF16)
        w_ref[rows[c], :] = sol[c][:, GDN_DV:GDN_DV + GDN_DK].astype(BF16)
        qe_ref[rows[c], :] = (qb[c].astype(F32) * eg_col[c]).astype(BF16)
        attn_ref[rows[c], :] = jnp.where(lower, qk[c] * decay[c], 0.0).astype(BF16)


def _intra(qn, kn, vn, kt, gate_rows, gate_flat, batch, seq):
    n = qn.shape[0]
    h = GDN_HEADS
    tt = min(2048, seq)
    nc = tt // CHUNK
    per_b = seq // tt
    wide = jax.ShapeDtypeStruct((n, h * LANES), BF16)
    row_spec = pl.BlockSpec((tt, LANES), lambda b, hh, t: (b * per_b + t, hh))
    return pl.pallas_call(
        functools.partial(_intra_body, nc=nc),
        out_shape=(wide, wide, wide,
                   jax.ShapeDtypeStruct((batch, h, LANES, seq), BF16),
                   jax.ShapeDtypeStruct((batch, h, seq, CHUNK), BF16)),
        grid=(batch, h, per_b),
        in_specs=[row_spec, row_spec, row_spec,
                  pl.BlockSpec((None, None, LANES, tt), lambda b, hh, t: (b, hh, 0, t)),
                  pl.BlockSpec((None, None, nc, CHUNK), lambda b, hh, t: (hh, b, t, 0)),
                  pl.BlockSpec((None, None, nc, CHUNK), lambda b, hh, t: (h + hh, b, t, 0)),
                  pl.BlockSpec((h, tt), lambda b, hh, t: (2, b * per_b + t))],
        out_specs=(row_spec, row_spec, row_spec,
                   pl.BlockSpec((None, None, LANES, tt), lambda b, hh, t: (b, hh, 0, t)),
                   pl.BlockSpec((None, None, tt, CHUNK), lambda b, hh, t: (b, hh, t, 0))),
        compiler_params=_params("arbitrary", "arbitrary", "arbitrary"),
        name="gdn_intra",
    )(qn, kn, vn, kt, gate_rows, gate_rows, gate_flat)


def _scan_body(u_ref, w_ref, qe_ref, kdt_ref, attn_ref, egl_ref, z_ref, gn_ref, o_ref, s_scr, *, nct):
    t = pl.program_id(1)

    @pl.when(t == 0)
    def _():
        s_scr[...] = jnp.zeros(s_scr.shape, F32)

    gn = gn_ref[...]
    hs = range(GDN_HEADS)
    cols = [slice(h * LANES, (h + 1) * LANES) for h in hs]
    state = [s_scr[h] for h in hs]
    for c in range(nct):
        rows = slice(c * CHUNK, (c + 1) * CHUNK)
        wq = [jnp.concatenate([w_ref[rows, cl], qe_ref[rows, cl]], axis=0) for cl in cols]
        r = [jnp.dot(a, s.astype(BF16), preferred_element_type=F32) for a, s in zip(wq, state)]
        v_new = [(u_ref[rows, cl].astype(F32) - x[0:CHUNK]).astype(BF16) for cl, x in zip(cols, r)]
        eg = [egl_ref[h, pl.ds(t * nct + c, 1), :][:, 0:1] for h in hs]
        state = [s * e + jnp.dot(kdt_ref[h, :, c * CHUNK:(c + 1) * CHUNK], v, preferred_element_type=F32)
                 for h, s, e, v in zip(hs, state, eg, v_new)]
        o = [x[CHUNK:2 * CHUNK] + jnp.dot(attn_ref[h, rows, :], v, preferred_element_type=F32)
             for h, x, v in zip(hs, r, v_new)]
        for h in hs:
            on = o[h] * lax.rsqrt(jnp.mean(o[h] * o[h], axis=-1, keepdims=True) + EPS) * gn
            o_ref[rows, cols[h]] = (on * _silu(z_ref[rows, cols[h]].astype(F32))).astype(BF16)
    for h in hs:
        s_scr[h] = state[h]


def _scan(u, w, qe, kdt, attn, egl, p_main, gdn_norm, batch, seq, z_blk):
    n = u.shape[0]
    h = GDN_HEADS
    tt = min(512, seq)
    nct = tt // CHUNK
    per_b = seq // tt
    wide_spec = pl.BlockSpec((tt, h * LANES), lambda b, t: (b * per_b + t, 0))
    return pl.pallas_call(
        functools.partial(_scan_body, nct=nct),
        out_shape=jax.ShapeDtypeStruct((n, h * LANES), BF16),
        grid=(batch, per_b),
        in_specs=[wide_spec, wide_spec, wide_spec,
                  pl.BlockSpec((None, h, LANES, tt), lambda b, t: (b, 0, 0, t)),
                  pl.BlockSpec((None, h, tt, CHUNK), lambda b, t: (b, 0, t, 0)),
                  pl.BlockSpec((h, None, seq // CHUNK, CHUNK), lambda b, t: (3, b, 0, 0)),
                  pl.BlockSpec((tt, h * LANES), lambda b, t: (b * per_b + t, z_blk)),
                  pl.BlockSpec((1, LANES), lambda b, t: (0, 0))],
        out_specs=wide_spec,
        scratch_shapes=[pltpu.VMEM((h, GDN_DK, GDN_DV), F32)],
        compiler_params=_params("arbitrary", "arbitrary"),
        name="gdn_scan",
    )(u, w, qe, kdt, attn, egl, p_main, gdn_norm)


def _merge_body(a_ref, b_ref, wa_ref, wb_ref, ga_ref, gb_ref, o_ref):
    ya = jnp.dot(a_ref[...], wa_ref[...], preferred_element_type=F32)
    yb = jnp.dot(b_ref[...], wb_ref[...], preferred_element_type=F32)
    o = _sigmoid(ga_ref[...].astype(F32)) * ya + _sigmoid(gb_ref[...].astype(F32)) * yb
    o_ref[...] = o.astype(o_ref.dtype)


def _merge(o_a, o_b, w_a, w_b, layer, p_main, d):
    n, ka = o_a.shape
    kb = o_b.shape[1]
    tm, tn = 1024, 1024
    nb = d // tn
    return pl.pallas_call(
        _merge_body,
        out_shape=jax.ShapeDtypeStruct((n, d), BF16),
        grid=(n // tm, nb),
        in_specs=[pl.BlockSpec((tm, ka), lambda i, j: (i, 0)),
                  pl.BlockSpec((tm, kb), lambda i, j: (i, 0)),
                  pl.BlockSpec((None, ka, tn), lambda i, j: (layer, 0, j)),
                  pl.BlockSpec((None, kb, tn), lambda i, j: (layer, 0, j)),
                  pl.BlockSpec((tm, tn), lambda i, j: (i, j)),
                  pl.BlockSpec((tm, tn), lambda i, j: (i, nb + j))],
        out_specs=pl.BlockSpec((tm, tn), lambda i, j: (i, j)),
        compiler_params=_params("arbitrary", "arbitrary"),
        name="branch_merge",
    )(o_a, o_b, w_a, w_b, p_main, p_main)


def _resid_body(a_ref, w_ref, x_ref, gt_ref, o_ref):
    y = jnp.dot(a_ref[...], w_ref[...], preferred_element_type=F32)
    o_ref[...] = x_ref[...] + gt_ref[...] * y


def _resid_matmul(a, w, layer, x, gt, seq, tm, tn):
    n, k = a.shape
    d = w.shape[-1]
    tm = min(tm, seq)
    per_b = seq // tm
    return pl.pallas_call(
        _resid_body,
        out_shape=jax.ShapeDtypeStruct((n, d), F32),
        grid=(n // tm, d // tn),
        in_specs=[pl.BlockSpec((tm, k), lambda i, j: (i, 0)),
                  pl.BlockSpec((None, k, tn), lambda i, j: (layer, 0, j)),
                  pl.BlockSpec((tm, tn), lambda i, j: (i, j)),
                  pl.BlockSpec((None, 1, tn), lambda i, j: (i // per_b, 0, j))],
        out_specs=pl.BlockSpec((tm, tn), lambda i, j: (i, j)),
        compiler_params=_params("arbitrary", "arbitrary"),
        name="resid_matmul",
    )(a, w, x, gt)


def _ffn_up_body(x_ref, nw_ref, sc_ref, sh_ref, wg_ref, wu_ref, o_ref, h_even, h_odd, *, n_tiles):
    def norm(fill):
        _norm_slice(x_ref, nw_ref, sc_ref, sh_ref, fill)

    def matmul(cur):
        hb = cur[...]
        gate = jnp.dot(hb, wg_ref[...], preferred_element_type=F32)
        up = jnp.dot(hb, wu_ref[...], preferred_element_type=F32)
        o_ref[...] = (_silu(gate) * up).astype(o_ref.dtype)

    _shifted_dispatch(n_tiles, h_even, h_odd, norm, matmul)


def _ffn_up(x, nw, sc, sh, w_gu, layer, seq):
    n, d = x.shape
    dff = w_gu.shape[-1] // 2
    tm, tn = min(1024, seq), 512
    n_tiles = n // tm
    nb = dff // tn
    in_specs, orow, col = _shifted_specs(tm, d, seq // tm, n_tiles)
    return pl.pallas_call(
        functools.partial(_ffn_up_body, n_tiles=n_tiles),
        out_shape=jax.ShapeDtypeStruct((n, dff), BF16),
        grid=(n_tiles + 1, nb),
        in_specs=in_specs + [pl.BlockSpec((None, d, tn), lambda i, j: (layer, 0, col(i, j))),
                             pl.BlockSpec((None, d, tn), lambda i, j: (layer, 0, nb + col(i, j)))],
        out_specs=pl.BlockSpec((tm, tn), lambda i, j: (orow(i), col(i, j))),
        scratch_shapes=[pltpu.VMEM((tm, d), BF16), pltpu.VMEM((tm, d), BF16)],
        compiler_params=_params("arbitrary", "arbitrary"),
        name="ffn_up",
    )(x, nw, sc, sh, w_gu, w_gu)


def _final_norm_body(x_ref, w_ref, o_ref):
    o_ref[...] = _rms(x_ref[...], w_ref[...])


def _final_norm(x, w):
    n, d = x.shape
    tm = 512
    return pl.pallas_call(
        _final_norm_body,
        out_shape=jax.ShapeDtypeStruct((n, d), F32),
        grid=(n // tm,),
        in_specs=[pl.BlockSpec((tm, d), lambda i: (i, 0)), pl.BlockSpec((1, d), lambda i: (0, 0))],
        out_specs=pl.BlockSpec((tm, d), lambda i: (i, 0)),
        compiler_params=_params("arbitrary"),
        name="final_norm",
    )(x, w)


def _layout_w_in(w_in, d):
    wt = jnp.swapaxes(w_in, 1, 2)
    o_kpe = Q_LORA + KV_LORA
    o_qkvz = o_kpe + QK_ROPE
    o_ba = o_qkvz + 2 * GDN_QK + 2 * GDN_V
    o_gates = o_ba + 2 * GDN_HEADS
    w_main = jnp.concatenate([wt[:, o_gates:o_gates + 2 * d], wt[:, o_qkvz:o_ba], wt[:, 0:o_kpe]],
                             axis=1).astype(BF16)
    half = QK_ROPE // 2
    zeros = lambda rows: jnp.zeros((wt.shape[0], rows, wt.shape[2]), wt.dtype)
    w_small = jnp.concatenate([wt[:, o_kpe:o_kpe + half], zeros(LANES // 2 - half),
                               wt[:, o_kpe + half:o_qkvz], zeros(LANES // 2 - half),
                               wt[:, o_ba:o_gates], zeros(LANES - 2 * GDN_HEADS)], axis=1).astype(BF16)
    return w_main, w_small


def _layout_w_uq(w_uq):
    depth, k, _ = w_uq.shape
    wt = jnp.swapaxes(w_uq, 1, 2).reshape(depth, MLA_HEADS, QK_NOPE + QK_ROPE, k)
    half = QK_ROPE // 2
    z = jnp.zeros((depth, MLA_HEADS, LANES // 2 - half, k), wt.dtype)
    wt = jnp.concatenate([wt[:, :, :QK_NOPE + half], z, wt[:, :, QK_NOPE + half:], z], axis=2)
    return wt.reshape(depth, MLA_HEADS * MLA_QK_PAD, k).astype(BF16)


def _rope_tables(positions):
    inv_freq = 1.0 / (ROPE_THETA ** (jnp.arange(0, QK_ROPE, 2, dtype=F32) / QK_ROPE))
    ang = positions.astype(F32).reshape(-1)[:, None] * inv_freq
    cos, sin = jnp.cos(ang), jnp.sin(ang)
    z = jnp.zeros_like(cos)
    return jnp.concatenate([cos, z, cos, z], axis=1), jnp.concatenate([-sin, z, sin, z], axis=1)


def kernel(x, c, positions, w_ada, b_ada, norm_mix, norm_ffn, w_in, q_a_norm, kv_a_norm, w_uq, w_ukv, w_o_mla,
           conv_w, A_log, dt_bias, gdn_norm, w_o_gdn, w_o, w_gate_up, w_down, final_norm):
    batch, seq, d = x.shape
    depth = w_ada.shape[0]
    n = batch * seq
    h = GDN_HEADS
    cos_t, sin_t = _rope_tables(positions)
    mod = _ada(c, w_ada, b_ada).reshape(depth, batch, 6, 1, d)
    xs = x.reshape(n, d)

    qkv_blk0 = 2 * d // LANES
    z_blk = (2 * d + 2 * GDN_QK + GDN_V) // (h * LANES)
    cq_blk = (2 * d + 2 * GDN_QK + 2 * GDN_V) // Q_LORA
    ckv_blk = cq_blk + 1

    w_main, w_small = _layout_w_in(w_in, d)
    w_uq_b = _layout_w_uq(w_uq)
    w_ukv_b = w_ukv.astype(BF16)
    w_o_mla_b = w_o_mla.astype(BF16)
    w_o_gdn_b = w_o_gdn.astype(BF16)
    w_o_b = w_o.astype(BF16)
    w_gate_up_b = w_gate_up.astype(BF16)
    w_down_b = w_down.astype(BF16)

    for l in range(depth):
        sh_a, sc_a, gt_a, sh_f, sc_f, gt_f = [mod[l, :, i] for i in range(6)]
        p_main, p_small = _inproj(xs, norm_mix[l].reshape(1, d), sc_a, sh_a, w_main, w_small, l, seq)

        q, kv, kpe = _mla_proj(p_main, p_small, cos_t, sin_t, q_a_norm[l].reshape(1, -1),
                               kv_a_norm[l].reshape(1, -1), w_uq_b, w_ukv_b, l, cq_blk, ckv_blk)
        o_a = _flash(q.reshape(batch, seq, -1), kv.reshape(batch, seq, -1), kpe.reshape(batch, seq, -1))
        o_a = o_a.reshape(n, -1)

        qn, kn, vn, kt = _conv(p_main, conv_w[l], batch, seq, qkv_blk0)
        gate_flat = _gates(p_small, A_log[l], dt_bias[l])
        gate_rows = gate_flat.reshape(4 * h, batch, seq // CHUNK, CHUNK)
        u, w, qe, kdt, attn = _intra(qn, kn, vn, kt, gate_rows, gate_flat, batch, seq)
        o_b = _scan(u, w, qe, kdt, attn, gate_rows, p_main, gdn_norm[l].reshape(1, -1), batch, seq, z_blk)

        merged = _merge(o_a, o_b, w_o_mla_b, w_o_gdn_b, l, p_main, d)
        xs = _resid_matmul(merged, w_o_b, l, xs, gt_a, seq, 1024, 1024)

        act = _ffn_up(xs, norm_ffn[l].reshape(1, d), sc_f, sh_f, w_gate_up_b, l, seq)
        xs = _resid_matmul(act, w_down_b, l, xs, gt_f, seq, 1024, 512)

    return _final_norm(xs, final_norm.reshape(1, d)).reshape(batch, seq, d)
```

```python
import functools
import math

import jax
import jax.numpy as jnp
from jax import lax
from jax.experimental import pallas as pl
from jax.experimental.pallas import tpu as pltpu

F32 = jnp.float32
BF16 = jnp.bfloat16

MLA_HEADS = 8
QK_NOPE = 128
QK_ROPE = 64
V_HEAD = 128
Q_LORA = 512
KV_LORA = 512
ROPE_THETA = 10000.0
GDN_HEADS = 8
GDN_DK = 128
GDN_DV = 128
CONV_WIDTH = 4
CHUNK = 64
EPS = 1e-6

LANES = 128
BF16_SUBLANES = 16
MLA_QK_PAD = 2 * LANES
VMEM_LIMIT_BYTES = 56 * 1024 * 1024

GDN_QK = GDN_HEADS * GDN_DK
GDN_V = GDN_HEADS * GDN_DV


def _params(*semantics):
    return pltpu.CompilerParams(dimension_semantics=semantics, vmem_limit_bytes=VMEM_LIMIT_BYTES)


def _sigmoid(x):
    return 1.0 / (1.0 + jnp.exp(-x))


def _silu(x):
    return x * _sigmoid(x)


def _rms(x, w):
    return x * lax.rsqrt(jnp.mean(x * x, axis=-1, keepdims=True) + EPS) * w


def _bdot(a, b):
    return jnp.dot(a.astype(BF16), b.astype(BF16), preferred_element_type=F32)


def _ada_body(c_ref, w_ref, b_ref, o_ref):
    c = c_ref[...]
    o_ref[...] = _bdot(_silu(c), w_ref[...]) + b_ref[...]


def _ada(c, w_ada, b_ada):
    depth, d, n6 = w_ada.shape
    b = c.shape[0]
    tn = 1024
    return pl.pallas_call(
        _ada_body,
        out_shape=jax.ShapeDtypeStruct((depth, b, n6), F32),
        grid=(depth, n6 // tn),
        in_specs=[pl.BlockSpec((b, d), lambda l, j: (0, 0)),
                  pl.BlockSpec((None, d, tn), lambda l, j: (l, 0, j)),
                  pl.BlockSpec((None, 1, tn), lambda l, j: (l, 0, j))],
        out_specs=pl.BlockSpec((None, b, tn), lambda l, j: (l, 0, j)),
        compiler_params=_params("arbitrary", "arbitrary"),
        name="ada_mod",
    )(c, w_ada, b_ada.reshape(depth, 1, n6))


NORM_SLICES = 8


def _norm_slice(x_ref, nw_ref, sc_ref, sh_ref, h_ref):
    j = pl.program_id(1)
    rs = x_ref.shape[0] // NORM_SLICES
    r0 = pl.multiple_of(jnp.minimum(j, NORM_SLICES - 1) * rs, rs)
    h = _rms(x_ref[pl.ds(r0, rs), :], nw_ref[...]) * (1.0 + sc_ref[...]) + sh_ref[...]
    h_ref[pl.ds(r0, rs), :] = h.astype(BF16)


def _shifted_dispatch(n_tiles, h_even, h_odd, norm, matmul):
    i = pl.program_id(0)
    bufs = ((h_even, h_odd), (h_odd, h_even))

    @pl.when(i == 0)
    def _():
        norm(h_even)

    for parity, (fill, cur) in enumerate(bufs):
        @pl.when((i % 2 == parity) & (i > 0) & (i < n_tiles))
        def _(fill=fill, cur=cur):
            norm(fill)
            matmul(cur)

    @pl.when(i == n_tiles)
    def _():
        matmul(bufs[n_tiles % 2][1])


def _shifted_specs(tm, d, per_b, n_tiles):
    def xrow(i):
        return jnp.minimum(i, n_tiles - 1)

    in_specs = [pl.BlockSpec((tm, d), lambda i, j: (xrow(i), 0)),
                pl.BlockSpec((1, d), lambda i, j: (0, 0)),
                pl.BlockSpec((None, 1, d), lambda i, j: (xrow(i) // per_b, 0, 0)),
                pl.BlockSpec((None, 1, d), lambda i, j: (xrow(i) // per_b, 0, 0))]

    def orow(i):
        return jnp.maximum(i - 1, 0)

    def col(i, j):
        return jnp.where(i > 0, j, 0)

    return in_specs, orow, col


def _inproj_body(x_ref, nw_ref, sc_ref, sh_ref, w_ref, ws_ref, o_ref, os_ref, h_even, h_odd, *, n_tiles):
    def norm(fill):
        _norm_slice(x_ref, nw_ref, sc_ref, sh_ref, fill)

    nt = (((1,), (1,)), ((), ()))

    def matmul(cur):
        o_ref[...] = lax.dot_general(cur[...], w_ref[0], nt, preferred_element_type=F32).astype(o_ref.dtype)

        @pl.when(pl.program_id(1) == 0)
        def _():
            os_ref[...] = lax.dot_general(cur[...], ws_ref[...], nt, preferred_element_type=F32)

    _shifted_dispatch(n_tiles, h_even, h_odd, norm, matmul)


def _inproj(x, nw, sc, sh, w_t, regions, w_small, layer, seq):
    n, d = x.shape
    nm = sum(rows for _, rows in regions)
    ns = w_small.shape[1]
    tm, tn = min(1024, seq), 1024
    n_tiles = n // tm
    in_specs, orow, col = _shifted_specs(tm, d, seq // tm, n_tiles)

    def w_row(t):
        row, first = 0, 0
        for src, rows in regions:
            assert rows % tn == 0
            assert src % BF16_SUBLANES == 0
            row = jnp.where(t >= first, src + (t - first) * tn, row)
            first += rows // tn
        return pl.multiple_of(row, BF16_SUBLANES)

    return pl.pallas_call(
        functools.partial(_inproj_body, n_tiles=n_tiles),
        out_shape=(jax.ShapeDtypeStruct((n, nm), BF16), jax.ShapeDtypeStruct((n, ns), F32)),
        grid=(n_tiles + 1, nm // tn),
        in_specs=in_specs + [pl.BlockSpec((pl.Element(1), pl.Element(tn), pl.Element(d)),
                                          lambda i, j: (layer, w_row(col(i, j)), 0)),
                             pl.BlockSpec((None, ns, d), lambda i, j: (layer, 0, 0))],
        out_specs=(pl.BlockSpec((tm, tn), lambda i, j: (orow(i), col(i, j))),
                   pl.BlockSpec((tm, ns), lambda i, j: (orow(i), 0))),
        scratch_shapes=[pltpu.VMEM((tm, d), BF16), pltpu.VMEM((tm, d), BF16)],
        compiler_params=_params("arbitrary", "arbitrary"),
        name="in_proj",
    )(x, nw, sc, sh, w_t, w_small)


def _rope128(x, cos, sin):
    return x * cos + pltpu.roll(x, LANES // 2, 1) * sin


def _mla_proj_body(cq_ref, ckv_ref, kpe_ref, cos_ref, sin_ref, qn_ref, kvn_ref, wq_ref, wkv_ref,
                   q_ref, kv_ref, kpeo_ref):
    cos = cos_ref[...]
    sin = sin_ref[...]
    scale = (QK_NOPE + QK_ROPE) ** -0.5 * math.log2(math.e)
    cqn = _rms(cq_ref[...].astype(F32), qn_ref[...]).astype(BF16)
    q = lax.dot_general(cqn, wq_ref[...], (((1,), (1,)), ((), ())), preferred_element_type=F32)
    for h in range(MLA_HEADS):
        lo = h * MLA_QK_PAD
        q_ref[:, lo:lo + LANES] = (q[:, lo:lo + LANES] * scale).astype(BF16)
        pe = _rope128(q[:, lo + LANES:lo + 2 * LANES], cos, sin)
        q_ref[:, lo + LANES:lo + 2 * LANES] = (pe * scale).astype(BF16)
    kv_ref[...] = _bdot(_rms(ckv_ref[...].astype(F32), kvn_ref[...]), wkv_ref[...]).astype(BF16)
    kpeo_ref[...] = _rope128(kpe_ref[...], cos, sin).astype(BF16)


def _mla_proj(p_main, p_small, cos_t, sin_t, qn, kvn, wq, wkv, layer, cq_blk, ckv_blk):
    n = p_main.shape[0]
    tm = 512
    nq, nkv = wq.shape[1], wkv.shape[-1]
    return pl.pallas_call(
        _mla_proj_body,
        out_shape=(jax.ShapeDtypeStruct((n, nq), BF16), jax.ShapeDtypeStruct((n, nkv), BF16),
                   jax.ShapeDtypeStruct((n, LANES), BF16)),
        grid=(n // tm,),
        in_specs=[pl.BlockSpec((tm, Q_LORA), lambda i: (i, cq_blk)),
                  pl.BlockSpec((tm, KV_LORA), lambda i: (i, ckv_blk)),
                  pl.BlockSpec((tm, LANES), lambda i: (i, 0)),
                  pl.BlockSpec((tm, LANES), lambda i: (i, 0)),
                  pl.BlockSpec((tm, LANES), lambda i: (i, 0)),
                  pl.BlockSpec((1, Q_LORA), lambda i: (0, 0)),
                  pl.BlockSpec((1, KV_LORA), lambda i: (0, 0)),
                  pl.BlockSpec((None, nq, Q_LORA), lambda i: (layer, 0, 0)),
                  pl.BlockSpec((None, KV_LORA, nkv), lambda i: (layer, 0, 0))],
        out_specs=(pl.BlockSpec((tm, nq), lambda i: (i, 0)),
                   pl.BlockSpec((tm, nkv), lambda i: (i, 0)),
                   pl.BlockSpec((tm, LANES), lambda i: (i, 0))),
        compiler_params=_params("arbitrary"),
        name="mla_proj",
    )(p_main, p_main, p_small, cos_t, sin_t, qn, kvn, wq, wkv)


def _flash_body(q_ref, kv_ref, kpe_ref, o_ref, *, tq, nq):
    tiles = [slice(i * tq, (i + 1) * tq) for i in range(nq)]
    row = lax.broadcasted_iota(jnp.int32, (tq, tq), 0)
    col = lax.broadcasted_iota(jnp.int32, (tq, tq), 1)
    causal = row >= col

    def scores(r):
        k = jnp.concatenate([kv_ref[tiles[r], 0:QK_NOPE], kpe_ref[tiles[r], :]], axis=1)
        return [lax.dot_general(q_ref[tiles[i], :], k, (((1,), (1,)), ((), ())), preferred_element_type=F32)
                for i in range(r, nq)]

    m = [None] * nq
    l = [None] * nq
    acc = [None] * nq
    s_next = scores(0)
    for r in range(nq):
        s_cur = s_next
        if r + 1 < nq:
            s_next = scores(r + 1)
        v = kv_ref[tiles[r], QK_NOPE:QK_NOPE + V_HEAD]
        p = []
        alpha = []
        for i, s in zip(range(r, nq), s_cur):
            if i == r:
                s = jnp.where(causal, s, -jnp.inf)
            s_max = jnp.max(s, axis=-1, keepdims=True)
            m_new = s_max if r == 0 else jnp.maximum(m[i], s_max)
            e = jnp.exp2(s - m_new)
            e_sum = jnp.sum(e, axis=-1, keepdims=True)
            if r == 0:
                alpha.append(None)
                l[i] = e_sum
            else:
                a = jnp.exp2(m[i] - m_new)
                alpha.append(a)
                l[i] = a * l[i] + e_sum
            m[i] = m_new
            p.append(e.astype(BF16))
        for i, pi, a in zip(range(r, nq), p, alpha):
            pv = jnp.dot(pi, v, preferred_element_type=F32)
            acc[i] = pv if a is None else a * acc[i] + pv
        o_ref[tiles[r], :] = (acc[r] / l[r]).astype(o_ref.dtype)


def _flash(q, kv, kpe):
    b, t, _ = q.shape
    tq = min(512, t)
    return pl.pallas_call(
        functools.partial(_flash_body, tq=tq, nq=t // tq),
        out_shape=jax.ShapeDtypeStruct((b, t, MLA_HEADS * V_HEAD), BF16),
        grid=(b, MLA_HEADS),
        in_specs=[pl.BlockSpec((None, t, MLA_QK_PAD), lambda bi, h: (bi, 0, h)),
                  pl.BlockSpec((None, t, QK_NOPE + V_HEAD), lambda bi, h: (bi, 0, h)),
                  pl.BlockSpec((None, t, LANES), lambda bi, h: (bi, 0, 0))],
        out_specs=pl.BlockSpec((None, t, V_HEAD), lambda bi, h: (bi, 0, h)),
        compiler_params=_params("arbitrary", "arbitrary"),
        name="mla_flash",
    )(q, kv, kpe)


def _conv_silu(x_ref, w_ref):
    u = x_ref[...].astype(F32)
    w = w_ref[...]
    sub = 8
    row = lax.broadcasted_iota(jnp.int32, (sub, u.shape[1]), 0)
    y = u * w[CONV_WIDTH - 1:CONV_WIDTH, :]
    for s in range(1, CONV_WIDTH):
        rolled = pltpu.roll(u, s, 0)
        shifted = jnp.concatenate([jnp.where(row >= s, rolled[0:sub], 0.0), rolled[sub:]], axis=0)
        y = y + shifted * w[CONV_WIDTH - 1 - s:CONV_WIDTH - s, :]
    return _silu(y)


def _l2norm(y):
    return y * lax.rsqrt(jnp.sum(y * y, axis=-1, keepdims=True) + EPS)


def _conv_body(xq_ref, xk_ref, xv_ref, wq_ref, wk_ref, wv_ref, q_ref, k_ref, v_ref, kt_ref):
    q_ref[...] = (_l2norm(_conv_silu(xq_ref, wq_ref)) * GDN_DK ** -0.5).astype(BF16)
    k = _l2norm(_conv_silu(xk_ref, wk_ref))
    k_ref[...] = k.astype(BF16)
    kt_ref[...] = k.T.astype(BF16)
    v_ref[...] = _conv_silu(xv_ref, wv_ref).astype(BF16)


def _conv(p_main, conv_w, batch, seq, col_blk0):
    n = p_main.shape[0]
    h = GDN_HEADS
    wide = jax.ShapeDtypeStruct((n, h * LANES), BF16)
    out_spec = pl.BlockSpec((seq, LANES), lambda b, hh: (b, hh))

    def x_spec(kind):
        return pl.BlockSpec((seq, LANES), lambda b, hh: (b, col_blk0 + kind * h + hh))

    def w_spec(kind):
        return pl.BlockSpec((CONV_WIDTH, LANES), lambda b, hh: (0, kind * h + hh))

    return pl.pallas_call(
        _conv_body,
        out_shape=(wide, wide, wide, jax.ShapeDtypeStruct((batch, h, LANES, seq), BF16)),
        grid=(batch, h),
        in_specs=[x_spec(0), x_spec(1), x_spec(2), w_spec(0), w_spec(1), w_spec(2)],
        out_specs=(out_spec, out_spec, out_spec,
                   pl.BlockSpec((None, None, LANES, seq), lambda b, hh: (b, hh, 0, 0))),
        compiler_params=_params("arbitrary", "arbitrary"),
        name="gdn_conv",
    )(p_main, p_main, p_main, conv_w, conv_w, conv_w)


def _split3(x):
    hi = x.astype(BF16).astype(F32)
    mid = (x - hi).astype(BF16).astype(F32)
    lo = x - hi - mid
    return hi, mid, lo


def _gate_body(s_ref, alog_ref, dtb_ref, o_ref):
    tm = s_ref.shape[0]
    h = GDN_HEADS
    t = s_ref[...].T
    beta = _sigmoid(t[0:h])
    a = t[h:2 * h] + dtb_ref[...]
    softplus = jnp.maximum(a, 0.0) + jnp.log(1.0 + jnp.exp(-jnp.abs(a)))
    g = -jnp.exp(alog_ref[...]) * softplus
    r = lax.broadcasted_iota(jnp.int32, (tm, tm), 0)
    c = lax.broadcasted_iota(jnp.int32, (tm, tm), 1)
    shift = CHUNK.bit_length() - 1
    same = lax.shift_right_logical(r, shift) == lax.shift_right_logical(c, shift)
    cum_m = jnp.where(same & (r <= c), 1.0, 0.0).astype(BF16)
    tot_m = jnp.where(same, 1.0, 0.0).astype(BF16)
    parts = jnp.concatenate(_split3(g), axis=0).astype(BF16)
    cum3 = jnp.dot(parts, cum_m, preferred_element_type=F32)
    tot3 = jnp.dot(parts, tot_m, preferred_element_type=F32)
    gc = cum3[0:h] + cum3[h:2 * h] + cum3[2 * h:3 * h]
    gl = tot3[0:h] + tot3[h:2 * h] + tot3[2 * h:3 * h]
    o_ref[0:h, :] = beta
    o_ref[h:2 * h, :] = gc
    o_ref[2 * h:3 * h, :] = jnp.exp(gl - gc)
    o_ref[3 * h:4 * h, :] = jnp.exp(gl)


def _gates(p_small, a_log, dt_bias):
    n = p_small.shape[0]
    tm = 512
    h = GDN_HEADS
    return pl.pallas_call(
        _gate_body,
        out_shape=jax.ShapeDtypeStruct((4 * h, n), F32),
        grid=(n // tm,),
        in_specs=[pl.BlockSpec((tm, LANES), lambda i: (i, 1)),
                  pl.BlockSpec((h, 1), lambda i: (0, 0)),
                  pl.BlockSpec((h, 1), lambda i: (0, 0))],
        out_specs=pl.BlockSpec((4 * h, tm), lambda i: (0, i)),
        compiler_params=_params("arbitrary"),
        name="gdn_gates",
    )(p_small, a_log.reshape(h, 1), dt_bias.reshape(h, 1))


def _intra_body(q_ref, k_ref, v_ref, kt_ref, beta_ref, gc_ref, dec_ref,
                u_ref, w_ref, qe_ref, kdt_ref, attn_ref, *, nc):
    hh = pl.program_id(1)
    ri = lax.broadcasted_iota(jnp.int32, (CHUNK, CHUNK), 0)
    ci = lax.broadcasted_iota(jnp.int32, (CHUNK, CHUNK), 1)
    eye = ri == ci
    lower = ri >= ci
    strict = ri > ci
    kt = kt_ref[...]
    kdt_ref[...] = (kt.astype(F32) * dec_ref[pl.ds(hh, 1), :]).astype(BF16)

    def to_col(rowvec):
        return jnp.sum(jnp.where(eye, jnp.broadcast_to(rowvec, (CHUNK, CHUNK)), 0.0), axis=1, keepdims=True)

    cs = range(nc)
    rows = [slice(c * CHUNK, (c + 1) * CHUNK) for c in cs]
    g_row = [gc_ref[c:c + 1, :] for c in cs]
    g_col = [to_col(g) for g in g_row]
    b_col = [to_col(beta_ref[c:c + 1, :]) for c in cs]
    eg_col = [jnp.exp(g) for g in g_col]
    decay = [jnp.where(lower, jnp.exp(jnp.where(lower, gc - gr, 0.0)), 0.0) for gc, gr in zip(g_col, g_row)]
    ktc = [kt[:, r] for r in rows]
    qb = [q_ref[r, :] for r in rows]
    kb = [k_ref[r, :].astype(F32) * b for r, b in zip(rows, b_col)]
    rhs = [jnp.concatenate([v_ref[r, :].astype(F32) * b, k * e], axis=1)
           for r, b, k, e in zip(rows, b_col, kb, eg_col)]
    kk = [_bdot(k, t) for k, t in zip(kb, ktc)]
    qk = [_bdot(q, t) for q, t in zip(qb, ktc)]
    lmat = [jnp.where(strict, k * d, 0.0) for k, d in zip(kk, decay)]
    xt = None
    s = 1
    while s < CHUNK:
        shift = s.bit_length() - 1
        bi = lax.shift_right_logical(ri, shift)
        bj = lax.shift_right_logical(ci, shift)
        pair = (lax.shift_right_logical(bi, 1) == lax.shift_right_logical(bj, 1)) & ((bi & 1) == 1) & ((bj & 1) == 0)
        cblk = [jnp.where(pair, m, 0.0) for m in lmat]
        if xt is None:
            xt = [-c for c in cblk]
        else:
            y = [c + _bdot(x, c) for x, c in zip(xt, cblk)]
            xt = [x - (a + _bdot(a, x)) for x, a in zip(xt, y)]
        s *= 2
    sol = [r + _bdot(x, r) for x, r in zip(xt, rhs)]
    for c in cs:
        u_ref[rows[c], :] = sol[c][:, 0:GDN_DV].astype(BF16)
        w_ref[rows[c], :] = sol[c][:, GDN_DV:GDN_DV + GDN_DK].astype(BF16)
        qe_ref[rows[c], :] = (qb[c].astype(F32) * eg_col[c]).astype(BF16)
        attn_ref[rows[c], :] = jnp.where(lower, qk[c] * decay[c], 0.0).astype(BF16)


def _intra(qn, kn, vn, kt, gate_rows, gate_flat, batch, seq):
    n = qn.shape[0]
    h = GDN_HEADS
    tt = min(2048, seq)
    nc = tt // CHUNK
    per_b = seq // tt
    wide = jax.ShapeDtypeStruct((n, h * LANES), BF16)
    row_spec = pl.BlockSpec((tt, LANES), lambda b, hh, t: (b * per_b + t, hh))
    return pl.pallas_call(
        functools.partial(_intra_body, nc=nc),
        out_shape=(wide, wide, wide,
                   jax.ShapeDtypeStruct((batch, h, LANES, seq), BF16),
                   jax.ShapeDtypeStruct((batch, h, seq, CHUNK), BF16)),
        grid=(batch, h, per_b),
        in_specs=[row_spec, row_spec, row_spec,
                  pl.BlockSpec((None, None, LANES, tt), lambda b, hh, t: (b, hh, 0, t)),
                  pl.BlockSpec((None, None, nc, CHUNK), lambda b, hh, t: (hh, b, t, 0)),
                  pl.BlockSpec((None, None, nc, CHUNK), lambda b, hh, t: (h + hh, b, t, 0)),
                  pl.BlockSpec((h, tt), lambda b, hh, t: (2, b * per_b + t))],
        out_specs=(row_spec, row_spec, row_spec,
                   pl.BlockSpec((None, None, LANES, tt), lambda b, hh, t: (b, hh, 0, t)),
                   pl.BlockSpec((None, None, tt, CHUNK), lambda b, hh, t: (b, hh, t, 0))),
        compiler_params=_params("arbitrary", "arbitrary", "arbitrary"),
        name="gdn_intra",
    )(qn, kn, vn, kt, gate_rows, gate_rows, gate_flat)


def _scan_body(u_ref, w_ref, qe_ref, kdt_ref, attn_ref, egl_ref, z_ref, gn_ref, o_ref, s_scr, *, nct):
    t = pl.program_id(1)

    @pl.when(t == 0)
    def _():
        s_scr[...] = jnp.zeros(s_scr.shape, F32)

    gn = gn_ref[...]
    hs = range(GDN_HEADS)
    cols = [slice(h * LANES, (h + 1) * LANES) for h in hs]
    state = [s_scr[h] for h in hs]
    for c in range(nct):
        rows = slice(c * CHUNK, (c + 1) * CHUNK)
        wq = [jnp.concatenate([w_ref[rows, cl], qe_ref[rows, cl]], axis=0) for cl in cols]
        r = [jnp.dot(a, s.astype(BF16), preferred_element_type=F32) for a, s in zip(wq, state)]
        v_new = [(u_ref[rows, cl].astype(F32) - x[0:CHUNK]).astype(BF16) for cl, x in zip(cols, r)]
        eg = [egl_ref[h, pl.ds(t * nct + c, 1), :][:, 0:1] for h in hs]
        state = [s * e + jnp.dot(kdt_ref[h, :, c * CHUNK:(c + 1) * CHUNK], v, preferred_element_type=F32)
                 for h, s, e, v in zip(hs, state, eg, v_new)]
        o = [x[CHUNK:2 * CHUNK] + jnp.dot(attn_ref[h, rows, :], v, preferred_element_type=F32)
             for h, x, v in zip(hs, r, v_new)]
        for h in hs:
            on = o[h] * lax.rsqrt(jnp.mean(o[h] * o[h], axis=-1, keepdims=True) + EPS) * gn
            o_ref[rows, cols[h]] = (on * _silu(z_ref[rows, cols[h]].astype(F32))).astype(BF16)
    for h in hs:
        s_scr[h] = state[h]


def _scan(u, w, qe, kdt, attn, egl, p_main, gdn_norm, batch, seq, z_blk):
    n = u.shape[0]
    h = GDN_HEADS
    tt = min(512, seq)
    nct = tt // CHUNK
    per_b = seq // tt
    wide_spec = pl.BlockSpec((tt, h * LANES), lambda b, t: (b * per_b + t, 0))
    return pl.pallas_call(
        functools.partial(_scan_body, nct=nct),
        out_shape=jax.ShapeDtypeStruct((n, h * LANES), BF16),
        grid=(batch, per_b),
        in_specs=[wide_spec, wide_spec, wide_spec,
                  pl.BlockSpec((None, h, LANES, tt), lambda b, t: (b, 0, 0, t)),
                  pl.BlockSpec((None, h, tt, CHUNK), lambda b, t: (b, 0, t, 0)),
                  pl.BlockSpec((h, None, seq // CHUNK, CHUNK), lambda b, t: (3, b, 0, 0)),
                  pl.BlockSpec((tt, h * LANES), lambda b, t: (b * per_b + t, z_blk)),
                  pl.BlockSpec((1, LANES), lambda b, t: (0, 0))],
        out_specs=wide_spec,
        scratch_shapes=[pltpu.VMEM((h, GDN_DK, GDN_DV), F32)],
        compiler_params=_params("arbitrary", "arbitrary"),
        name="gdn_scan",
    )(u, w, qe, kdt, attn, egl, p_main, gdn_norm)


def _merge_body(a_ref, b_ref, wa_ref, wb_ref, ga_ref, gb_ref, o_ref):
    ya = jnp.dot(a_ref[...], wa_ref[...], preferred_element_type=F32)
    yb = jnp.dot(b_ref[...], wb_ref[...], preferred_element_type=F32)
    o = _sigmoid(ga_ref[...].astype(F32)) * ya + _sigmoid(gb_ref[...].astype(F32)) * yb
    o_ref[...] = o.astype(o_ref.dtype)


def _merge(o_a, o_b, w_a, w_b, layer, p_main, d):
    n, ka = o_a.shape
    kb = o_b.shape[1]
    tm, tn = 1024, 1024
    nb = d // tn
    return pl.pallas_call(
        _merge_body,
        out_shape=jax.ShapeDtypeStruct((n, d), BF16),
        grid=(n // tm, nb),
        in_specs=[pl.BlockSpec((tm, ka), lambda i, j: (i, 0)),
                  pl.BlockSpec((tm, kb), lambda i, j: (i, 0)),
                  pl.BlockSpec((None, ka, tn), lambda i, j: (layer, 0, j)),
                  pl.BlockSpec((None, kb, tn), lambda i, j: (layer, 0, j)),
                  pl.BlockSpec((tm, tn), lambda i, j: (i, j)),
                  pl.BlockSpec((tm, tn), lambda i, j: (i, nb + j))],
        out_specs=pl.BlockSpec((tm, tn), lambda i, j: (i, j)),
        compiler_params=_params("arbitrary", "arbitrary"),
        name="branch_merge",
    )(o_a, o_b, w_a, w_b, p_main, p_main)


def _resid_body(a_ref, w_ref, x_ref, gt_ref, o_ref):
    y = jnp.dot(a_ref[...], w_ref[...], preferred_element_type=F32)
    o_ref[...] = x_ref[...] + gt_ref[...] * y


def _resid_matmul(a, w, layer, x, gt, seq, tm, tn):
    n, k = a.shape
    d = w.shape[-1]
    tm = min(tm, seq)
    per_b = seq // tm
    return pl.pallas_call(
        _resid_body,
        out_shape=jax.ShapeDtypeStruct((n, d), F32),
        grid=(n // tm, d // tn),
        in_specs=[pl.BlockSpec((tm, k), lambda i, j: (i, 0)),
                  pl.BlockSpec((None, k, tn), lambda i, j: (layer, 0, j)),
                  pl.BlockSpec((tm, tn), lambda i, j: (i, j)),
                  pl.BlockSpec((None, 1, tn), lambda i, j: (i // per_b, 0, j))],
        out_specs=pl.BlockSpec((tm, tn), lambda i, j: (i, j)),
        compiler_params=_params("arbitrary", "arbitrary"),
        name="resid_matmul",
    )(a, w, x, gt)


def _ffn_up_body(x_ref, nw_ref, sc_ref, sh_ref, wg_ref, wu_ref, o_ref, h_even, h_odd, *, n_tiles):
    def norm(fill):
        _norm_slice(x_ref, nw_ref, sc_ref, sh_ref, fill)

    def matmul(cur):
        hb = cur[...]
        gate = jnp.dot(hb, wg_ref[...], preferred_element_type=F32)
        up = jnp.dot(hb, wu_ref[...], preferred_element_type=F32)
        o_ref[...] = (_silu(gate) * up).astype(o_ref.dtype)

    _shifted_dispatch(n_tiles, h_even, h_odd, norm, matmul)


def _ffn_up(x, nw, sc, sh, w_gu, layer, seq):
    n, d = x.shape
    dff = w_gu.shape[-1] // 2
    tm, tn = min(1024, seq), 512
    n_tiles = n // tm
    nb = dff // tn
    in_specs, orow, col = _shifted_specs(tm, d, seq // tm, n_tiles)
    return pl.pallas_call(
        functools.partial(_ffn_up_body, n_tiles=n_tiles),
        out_shape=jax.ShapeDtypeStruct((n, dff), BF16),
        grid=(n_tiles + 1, nb),
        in_specs=in_specs + [pl.BlockSpec((None, d, tn), lambda i, j: (layer, 0, col(i, j))),
                             pl.BlockSpec((None, d, tn), lambda i, j: (layer, 0, nb + col(i, j)))],
        out_specs=pl.BlockSpec((tm, tn), lambda i, j: (orow(i), col(i, j))),
        scratch_shapes=[pltpu.VMEM((tm, d), BF16), pltpu.VMEM((tm, d), BF16)],
        compiler_params=_params("arbitrary", "arbitrary"),
        name="ffn_up",
    )(x, nw, sc, sh, w_gu, w_gu)


def _final_norm_body(x_ref, w_ref, o_ref):
    o_ref[...] = _rms(x_ref[...], w_ref[...])


def _final_norm(x, w):
    n, d = x.shape
    tm = 512
    return pl.pallas_call(
        _final_norm_body,
        out_shape=jax.ShapeDtypeStruct((n, d), F32),
        grid=(n // tm,),
        in_specs=[pl.BlockSpec((tm, d), lambda i: (i, 0)), pl.BlockSpec((1, d), lambda i: (0, 0))],
        out_specs=pl.BlockSpec((tm, d), lambda i: (i, 0)),
        compiler_params=_params("arbitrary"),
        name="final_norm",
    )(x, w)


def _layout_w_in(w_in, d):
    wt = jnp.swapaxes(w_in, 1, 2).astype(BF16)
    o_kpe = Q_LORA + KV_LORA
    o_qkvz = o_kpe + QK_ROPE
    o_ba = o_qkvz + 2 * GDN_QK + 2 * GDN_V
    o_gates = o_ba + 2 * GDN_HEADS
    regions = [(o_gates, 2 * d), (o_qkvz, o_ba - o_qkvz), (0, o_kpe)]
    half = QK_ROPE // 2
    zeros = lambda rows: jnp.zeros((wt.shape[0], rows, wt.shape[2]), wt.dtype)
    w_small = jnp.concatenate([wt[:, o_kpe:o_kpe + half], zeros(LANES // 2 - half),
                               wt[:, o_kpe + half:o_qkvz], zeros(LANES // 2 - half),
                               wt[:, o_ba:o_gates], zeros(LANES - 2 * GDN_HEADS)], axis=1)
    return wt, regions, w_small


def _layout_w_uq(w_uq):
    depth, k, _ = w_uq.shape
    wt = jnp.swapaxes(w_uq, 1, 2).reshape(depth, MLA_HEADS, QK_NOPE + QK_ROPE, k)
    half = QK_ROPE // 2
    z = jnp.zeros((depth, MLA_HEADS, LANES // 2 - half, k), wt.dtype)
    wt = jnp.concatenate([wt[:, :, :QK_NOPE + half], z, wt[:, :, QK_NOPE + half:], z], axis=2)
    return wt.reshape(depth, MLA_HEADS * MLA_QK_PAD, k).astype(BF16)


def _rope_tables(positions):
    inv_freq = 1.0 / (ROPE_THETA ** (jnp.arange(0, QK_ROPE, 2, dtype=F32) / QK_ROPE))
    ang = positions.astype(F32).reshape(-1)[:, None] * inv_freq
    cos, sin = jnp.cos(ang), jnp.sin(ang)
    z = jnp.zeros_like(cos)
    return jnp.concatenate([cos, z, cos, z], axis=1), jnp.concatenate([-sin, z, sin, z], axis=1)


def kernel(x, c, positions, w_ada, b_ada, norm_mix, norm_ffn, w_in, q_a_norm, kv_a_norm, w_uq, w_ukv, w_o_mla,
           conv_w, A_log, dt_bias, gdn_norm, w_o_gdn, w_o, w_gate_up, w_down, final_norm):
    batch, seq, d = x.shape
    depth = w_ada.shape[0]
    n = batch * seq
    h = GDN_HEADS
    cos_t, sin_t = _rope_tables(positions)
    mod = _ada(c, w_ada, b_ada).reshape(depth, batch, 6, 1, d)
    xs = x.reshape(n, d)

    qkv_blk0 = 2 * d // LANES
    z_blk = (2 * d + 2 * GDN_QK + GDN_V) // (h * LANES)
    cq_blk = (2 * d + 2 * GDN_QK + 2 * GDN_V) // Q_LORA
    ckv_blk = cq_blk + 1

    w_in_t, in_regions, w_small = _layout_w_in(w_in, d)
    w_uq_b = _layout_w_uq(w_uq)
    w_ukv_b = w_ukv.astype(BF16)
    w_o_mla_b = w_o_mla.astype(BF16)
    w_o_gdn_b = w_o_gdn.astype(BF16)
    w_o_b = w_o.astype(BF16)
    w_gate_up_b = w_gate_up.astype(BF16)
    w_down_b = w_down.astype(BF16)

    for l in range(depth):
        sh_a, sc_a, gt_a, sh_f, sc_f, gt_f = [mod[l, :, i] for i in range(6)]
        p_main, p_small = _inproj(xs, norm_mix[l].reshape(1, d), sc_a, sh_a, w_in_t, in_regions, w_small, l, seq)

        q, kv, kpe = _mla_proj(p_main, p_small, cos_t, sin_t, q_a_norm[l].reshape(1, -1),
                               kv_a_norm[l].reshape(1, -1), w_uq_b, w_ukv_b, l, cq_blk, ckv_blk)
        o_a = _flash(q.reshape(batch, seq, -1), kv.reshape(batch, seq, -1), kpe.reshape(batch, seq, -1))
        o_a = o_a.reshape(n, -1)

        qn, kn, vn, kt = _conv(p_main, conv_w[l], batch, seq, qkv_blk0)
        gate_flat = _gates(p_small, A_log[l], dt_bias[l])
        gate_rows = gate_flat.reshape(4 * h, batch, seq // CHUNK, CHUNK)
        u, w, qe, kdt, attn = _intra(qn, kn, vn, kt, gate_rows, gate_flat, batch, seq)
        o_b = _scan(u, w, qe, kdt, attn, gate_rows, p_main, gdn_norm[l].reshape(1, -1), batch, seq, z_blk)

        merged = _merge(o_a, o_b, w_o_mla_b, w_o_gdn_b, l, p_main, d)
        xs = _resid_matmul(merged, w_o_b, l, xs, gt_a, seq, 1024, 1024)

        act = _ffn_up(xs, norm_ffn[l].reshape(1, d), sc_f, sh_f, w_gate_up_b, l, seq)
        xs = _resid_matmul(act, w_down_b, l, xs, gt_f, seq, 1024, 512)

    return _final_norm(xs, final_norm.reshape(1, d)).reshape(batch, seq, d)
```

```python
import functools
import math

import jax
import jax.numpy as jnp
from jax import lax
from jax.experimental import pallas as pl
from jax.experimental.pallas import tpu as pltpu

F32 = jnp.float32
BF16 = jnp.bfloat16

MLA_HEADS = 8
QK_NOPE = 128
QK_ROPE = 64
V_HEAD = 128
Q_LORA = 512
KV_LORA = 512
ROPE_THETA = 10000.0
GDN_HEADS = 8
GDN_DK = 128
GDN_DV = 128
CONV_WIDTH = 4
CHUNK = 64
EPS = 1e-6

LANES = 128
BF16_SUBLANES = 16
MLA_QK_PAD = 2 * LANES
VMEM_LIMIT_BYTES = 56 * 1024 * 1024

GDN_QK = GDN_HEADS * GDN_DK
GDN_V = GDN_HEADS * GDN_DV


def _params(*semantics):
    return pltpu.CompilerParams(dimension_semantics=semantics, vmem_limit_bytes=VMEM_LIMIT_BYTES)


def _sigmoid(x):
    return 1.0 / (1.0 + jnp.exp(-x))


def _silu(x):
    return x * _sigmoid(x)


def _rms(x, w):
    return x * lax.rsqrt(jnp.mean(x * x, axis=-1, keepdims=True) + EPS) * w


def _bdot(a, b):
    return jnp.dot(a.astype(BF16), b.astype(BF16), preferred_element_type=F32)


def _ada_body(c_ref, w_ref, b_ref, o_ref):
    c = c_ref[...]
    o_ref[...] = _bdot(_silu(c), w_ref[...]) + b_ref[...]


def _ada(c, w_ada, b_ada):
    depth, d, n6 = w_ada.shape
    b = c.shape[0]
    tn = 1024
    return pl.pallas_call(
        _ada_body,
        out_shape=jax.ShapeDtypeStruct((depth, b, n6), F32),
        grid=(depth, n6 // tn),
        in_specs=[pl.BlockSpec((b, d), lambda l, j: (0, 0)),
                  pl.BlockSpec((None, d, tn), lambda l, j: (l, 0, j)),
                  pl.BlockSpec((None, 1, tn), lambda l, j: (l, 0, j))],
        out_specs=pl.BlockSpec((None, b, tn), lambda l, j: (l, 0, j)),
        compiler_params=_params("arbitrary", "arbitrary"),
        name="ada_mod",
    )(c, w_ada, b_ada.reshape(depth, 1, n6))


NORM_SLICES = 8


def _norm_slice(x_ref, nw_ref, sc_ref, sh_ref, h_ref):
    j = pl.program_id(1)
    rs = x_ref.shape[0] // NORM_SLICES
    r0 = pl.multiple_of(jnp.minimum(j, NORM_SLICES - 1) * rs, rs)
    h = _rms(x_ref[pl.ds(r0, rs), :], nw_ref[...]) * (1.0 + sc_ref[...]) + sh_ref[...]
    h_ref[pl.ds(r0, rs), :] = h.astype(BF16)


def _shifted_dispatch(n_tiles, h_even, h_odd, norm, matmul):
    i = pl.program_id(0)
    bufs = ((h_even, h_odd), (h_odd, h_even))

    @pl.when(i == 0)
    def _():
        norm(h_even)

    for parity, (fill, cur) in enumerate(bufs):
        @pl.when((i % 2 == parity) & (i > 0) & (i < n_tiles))
        def _(fill=fill, cur=cur):
            norm(fill)
            matmul(cur)

    @pl.when(i == n_tiles)
    def _():
        matmul(bufs[n_tiles % 2][1])


def _shifted_specs(tm, d, per_b, n_tiles):
    def xrow(i):
        return jnp.minimum(i, n_tiles - 1)

    in_specs = [pl.BlockSpec((tm, d), lambda i, j: (xrow(i), 0)),
                pl.BlockSpec((1, d), lambda i, j: (0, 0)),
                pl.BlockSpec((None, 1, d), lambda i, j: (xrow(i) // per_b, 0, 0)),
                pl.BlockSpec((None, 1, d), lambda i, j: (xrow(i) // per_b, 0, 0))]

    def orow(i):
        return jnp.maximum(i - 1, 0)

    def col(i, j):
        return jnp.where(i > 0, j, 0)

    return in_specs, orow, col


def _inproj_body(x_ref, nw_ref, sc_ref, sh_ref, w_ref, ws_ref, o_ref, os_ref, h_even, h_odd, *, n_tiles):
    def norm(fill):
        _norm_slice(x_ref, nw_ref, sc_ref, sh_ref, fill)

    nt = (((1,), (1,)), ((), ()))

    def matmul(cur):
        o_ref[...] = lax.dot_general(cur[...], w_ref[0], nt, preferred_element_type=F32).astype(o_ref.dtype)

        @pl.when(pl.program_id(1) == 0)
        def _():
            os_ref[...] = lax.dot_general(cur[...], ws_ref[...], nt, preferred_element_type=F32)

    _shifted_dispatch(n_tiles, h_even, h_odd, norm, matmul)


def _inproj(x, nw, sc, sh, w_t, regions, w_small, layer, seq):
    n, d = x.shape
    nm = sum(rows for _, rows in regions)
    ns = w_small.shape[1]
    tm, tn = min(1024, seq), 1024
    n_tiles = n // tm
    in_specs, orow, col = _shifted_specs(tm, d, seq // tm, n_tiles)

    def w_row(t):
        row, first = 0, 0
        for src, rows in regions:
            assert rows % tn == 0
            assert src % BF16_SUBLANES == 0
            row = jnp.where(t >= first, src + (t - first) * tn, row)
            first += rows // tn
        return pl.multiple_of(row, BF16_SUBLANES)

    return pl.pallas_call(
        functools.partial(_inproj_body, n_tiles=n_tiles),
        out_shape=(jax.ShapeDtypeStruct((n, nm), BF16), jax.ShapeDtypeStruct((n, ns), F32)),
        grid=(n_tiles + 1, nm // tn),
        in_specs=in_specs + [pl.BlockSpec((pl.Element(1), pl.Element(tn), pl.Element(d)),
                                          lambda i, j: (layer, w_row(col(i, j)), 0)),
                             pl.BlockSpec((None, ns, d), lambda i, j: (layer, 0, 0))],
        out_specs=(pl.BlockSpec((tm, tn), lambda i, j: (orow(i), col(i, j))),
                   pl.BlockSpec((tm, ns), lambda i, j: (orow(i), 0))),
        scratch_shapes=[pltpu.VMEM((tm, d), BF16), pltpu.VMEM((tm, d), BF16)],
        compiler_params=_params("arbitrary", "arbitrary"),
        name="in_proj",
    )(x, nw, sc, sh, w_t, w_small)


def _rope128(x, cos, sin):
    return x * cos + pltpu.roll(x, LANES // 2, 1) * sin


def _mla_proj_body(cq_ref, ckv_ref, kpe_ref, cos_ref, sin_ref, qn_ref, kvn_ref, wq_ref, wkv_ref,
                   q_ref, kv_ref, kpeo_ref):
    cos = cos_ref[...]
    sin = sin_ref[...]
    scale = (QK_NOPE + QK_ROPE) ** -0.5 * math.log2(math.e)
    cqn = _rms(cq_ref[...].astype(F32), qn_ref[...]).astype(BF16)
    q = lax.dot_general(cqn, wq_ref[...], (((1,), (1,)), ((), ())), preferred_element_type=F32)
    for h in range(MLA_HEADS):
        lo = h * MLA_QK_PAD
        q_ref[:, lo:lo + LANES] = (q[:, lo:lo + LANES] * scale).astype(BF16)
        pe = _rope128(q[:, lo + LANES:lo + 2 * LANES], cos, sin)
        q_ref[:, lo + LANES:lo + 2 * LANES] = (pe * scale).astype(BF16)
    kv_ref[...] = _bdot(_rms(ckv_ref[...].astype(F32), kvn_ref[...]), wkv_ref[...]).astype(BF16)
    kpeo_ref[...] = _rope128(kpe_ref[...], cos, sin).astype(BF16)


def _mla_proj(p_main, p_small, cos_t, sin_t, qn, kvn, wq, wkv, layer, cq_blk, ckv_blk):
    n = p_main.shape[0]
    tm = 512
    nq, nkv = wq.shape[1], wkv.shape[-1]
    return pl.pallas_call(
        _mla_proj_body,
        out_shape=(jax.ShapeDtypeStruct((n, nq), BF16), jax.ShapeDtypeStruct((n, nkv), BF16),
                   jax.ShapeDtypeStruct((n, LANES), BF16)),
        grid=(n // tm,),
        in_specs=[pl.BlockSpec((tm, Q_LORA), lambda i: (i, cq_blk)),
                  pl.BlockSpec((tm, KV_LORA), lambda i: (i, ckv_blk)),
                  pl.BlockSpec((tm, LANES), lambda i: (i, 0)),
                  pl.BlockSpec((tm, LANES), lambda i: (i, 0)),
                  pl.BlockSpec((tm, LANES), lambda i: (i, 0)),
                  pl.BlockSpec((1, Q_LORA), lambda i: (0, 0)),
                  pl.BlockSpec((1, KV_LORA), lambda i: (0, 0)),
                  pl.BlockSpec((None, nq, Q_LORA), lambda i: (layer, 0, 0)),
                  pl.BlockSpec((None, KV_LORA, nkv), lambda i: (layer, 0, 0))],
        out_specs=(pl.BlockSpec((tm, nq), lambda i: (i, 0)),
                   pl.BlockSpec((tm, nkv), lambda i: (i, 0)),
                   pl.BlockSpec((tm, LANES), lambda i: (i, 0))),
        compiler_params=_params("arbitrary"),
        name="mla_proj",
    )(p_main, p_main, p_small, cos_t, sin_t, qn, kvn, wq, wkv)


def _flash_body(q_ref, kv_ref, kpe_ref, o_ref, *, tq, nq):
    tiles = [slice(i * tq, (i + 1) * tq) for i in range(nq)]
    row = lax.broadcasted_iota(jnp.int32, (tq, tq), 0)
    col = lax.broadcasted_iota(jnp.int32, (tq, tq), 1)
    causal = row >= col

    def scores(r):
        k = jnp.concatenate([kv_ref[tiles[r], 0:QK_NOPE], kpe_ref[tiles[r], :]], axis=1)
        return [lax.dot_general(q_ref[tiles[i], :], k, (((1,), (1,)), ((), ())), preferred_element_type=F32)
                for i in range(r, nq)]

    m = [None] * nq
    l = [None] * nq
    acc = [None] * nq
    s_next = scores(0)
    for r in range(nq):
        s_cur = s_next
        if r + 1 < nq:
            s_next = scores(r + 1)
        v = kv_ref[tiles[r], QK_NOPE:QK_NOPE + V_HEAD]
        p = []
        alpha = []
        for i, s in zip(range(r, nq), s_cur):
            if i == r:
                s = jnp.where(causal, s, -jnp.inf)
            s_max = jnp.max(s, axis=-1, keepdims=True)
            m_new = s_max if r == 0 else jnp.maximum(m[i], s_max)
            e = jnp.exp2(s - m_new)
            e_sum = jnp.sum(e, axis=-1, keepdims=True)
            if r == 0:
                alpha.append(None)
                l[i] = e_sum
            else:
                a = jnp.exp2(m[i] - m_new)
                alpha.append(a)
                l[i] = a * l[i] + e_sum
            m[i] = m_new
            p.append(e.astype(BF16))
        for i, pi, a in zip(range(r, nq), p, alpha):
            pv = jnp.dot(pi, v, preferred_element_type=F32)
            acc[i] = pv if a is None else a * acc[i] + pv
        o_ref[tiles[r], :] = (acc[r] / l[r]).astype(o_ref.dtype)


def _flash(q, kv, kpe):
    b, t, _ = q.shape
    tq = min(512, t)
    return pl.pallas_call(
        functools.partial(_flash_body, tq=tq, nq=t // tq),
        out_shape=jax.ShapeDtypeStruct((b, t, MLA_HEADS * V_HEAD), BF16),
        grid=(b, MLA_HEADS),
        in_specs=[pl.BlockSpec((None, t, MLA_QK_PAD), lambda bi, h: (bi, 0, h)),
                  pl.BlockSpec((None, t, QK_NOPE + V_HEAD), lambda bi, h: (bi, 0, h)),
                  pl.BlockSpec((None, t, LANES), lambda bi, h: (bi, 0, 0))],
        out_specs=pl.BlockSpec((None, t, V_HEAD), lambda bi, h: (bi, 0, h)),
        compiler_params=_params("arbitrary", "arbitrary"),
        name="mla_flash",
    )(q, kv, kpe)


def _conv_silu(x_ref, w_ref):
    u = x_ref[...].astype(F32)
    w = w_ref[...]
    sub = 8
    row = lax.broadcasted_iota(jnp.int32, (sub, u.shape[1]), 0)
    y = u * w[CONV_WIDTH - 1:CONV_WIDTH, :]
    for s in range(1, CONV_WIDTH):
        rolled = pltpu.roll(u, s, 0)
        shifted = jnp.concatenate([jnp.where(row >= s, rolled[0:sub], 0.0), rolled[sub:]], axis=0)
        y = y + shifted * w[CONV_WIDTH - 1 - s:CONV_WIDTH - s, :]
    return _silu(y)


def _l2norm(y):
    return y * lax.rsqrt(jnp.sum(y * y, axis=-1, keepdims=True) + EPS)


def _conv_body(xq_ref, xk_ref, xv_ref, wq_ref, wk_ref, wv_ref, q_ref, k_ref, v_ref, kt_ref):
    q_ref[...] = (_l2norm(_conv_silu(xq_ref, wq_ref)) * GDN_DK ** -0.5).astype(BF16)
    k = _l2norm(_conv_silu(xk_ref, wk_ref))
    k_ref[...] = k.astype(BF16)
    kt_ref[...] = k.T.astype(BF16)
    v_ref[...] = _conv_silu(xv_ref, wv_ref).astype(BF16)


def _conv(p_main, conv_w, batch, seq, col_blk0):
    n = p_main.shape[0]
    h = GDN_HEADS
    wide = jax.ShapeDtypeStruct((n, h * LANES), BF16)
    out_spec = pl.BlockSpec((seq, LANES), lambda b, hh: (b, hh))

    def x_spec(kind):
        return pl.BlockSpec((seq, LANES), lambda b, hh: (b, col_blk0 + kind * h + hh))

    def w_spec(kind):
        return pl.BlockSpec((CONV_WIDTH, LANES), lambda b, hh: (0, kind * h + hh))

    return pl.pallas_call(
        _conv_body,
        out_shape=(wide, wide, wide, jax.ShapeDtypeStruct((batch, h, LANES, seq), BF16)),
        grid=(batch, h),
        in_specs=[x_spec(0), x_spec(1), x_spec(2), w_spec(0), w_spec(1), w_spec(2)],
        out_specs=(out_spec, out_spec, out_spec,
                   pl.BlockSpec((None, None, LANES, seq), lambda b, hh: (b, hh, 0, 0))),
        compiler_params=_params("arbitrary", "arbitrary"),
        name="gdn_conv",
    )(p_main, p_main, p_main, conv_w, conv_w, conv_w)


def _split3(x):
    hi = x.astype(BF16).astype(F32)
    mid = (x - hi).astype(BF16).astype(F32)
    lo = x - hi - mid
    return hi, mid, lo


def _gate_body(s_ref, alog_ref, dtb_ref, o_ref):
    tm = s_ref.shape[0]
    h = GDN_HEADS
    t = s_ref[...].T
    beta = _sigmoid(t[0:h])
    a = t[h:2 * h] + dtb_ref[...]
    softplus = jnp.maximum(a, 0.0) + jnp.log(1.0 + jnp.exp(-jnp.abs(a)))
    g = -jnp.exp(alog_ref[...]) * softplus
    r = lax.broadcasted_iota(jnp.int32, (tm, tm), 0)
    c = lax.broadcasted_iota(jnp.int32, (tm, tm), 1)
    shift = CHUNK.bit_length() - 1
    same = lax.shift_right_logical(r, shift) == lax.shift_right_logical(c, shift)
    cum_m = jnp.where(same & (r <= c), 1.0, 0.0).astype(BF16)
    tot_m = jnp.where(same, 1.0, 0.0).astype(BF16)
    parts = jnp.concatenate(_split3(g), axis=0).astype(BF16)
    cum3 = jnp.dot(parts, cum_m, preferred_element_type=F32)
    tot3 = jnp.dot(parts, tot_m, preferred_element_type=F32)
    gc = cum3[0:h] + cum3[h:2 * h] + cum3[2 * h:3 * h]
    gl = tot3[0:h] + tot3[h:2 * h] + tot3[2 * h:3 * h]
    o_ref[0:h, :] = beta
    o_ref[h:2 * h, :] = gc
    o_ref[2 * h:3 * h, :] = jnp.exp(gl - gc)
    o_ref[3 * h:4 * h, :] = jnp.exp(gl)


def _gates(p_small, a_log, dt_bias):
    n = p_small.shape[0]
    tm = 512
    h = GDN_HEADS
    return pl.pallas_call(
        _gate_body,
        out_shape=jax.ShapeDtypeStruct((4 * h, n), F32),
        grid=(n // tm,),
        in_specs=[pl.BlockSpec((tm, LANES), lambda i: (i, 1)),
                  pl.BlockSpec((h, 1), lambda i: (0, 0)),
                  pl.BlockSpec((h, 1), lambda i: (0, 0))],
        out_specs=pl.BlockSpec((4 * h, tm), lambda i: (0, i)),
        compiler_params=_params("arbitrary"),
        name="gdn_gates",
    )(p_small, a_log.reshape(h, 1), dt_bias.reshape(h, 1))


def _intra_body(q_ref, k_ref, v_ref, kt_ref, beta_ref, gc_ref, dec_ref,
                u_ref, w_ref, qe_ref, kdt_ref, attn_ref, *, nc):
    hh = pl.program_id(1)
    ri = lax.broadcasted_iota(jnp.int32, (CHUNK, CHUNK), 0)
    ci = lax.broadcasted_iota(jnp.int32, (CHUNK, CHUNK), 1)
    eye = ri == ci
    lower = ri >= ci
    strict = ri > ci
    kt = kt_ref[...]
    kdt_ref[...] = (kt.astype(F32) * dec_ref[pl.ds(hh, 1), :]).astype(BF16)

    def to_col(rowvec):
        return jnp.sum(jnp.where(eye, jnp.broadcast_to(rowvec, (CHUNK, CHUNK)), 0.0), axis=1, keepdims=True)

    cs = range(nc)
    rows = [slice(c * CHUNK, (c + 1) * CHUNK) for c in cs]
    g_row = [gc_ref[c:c + 1, :] for c in cs]
    g_col = [to_col(g) for g in g_row]
    b_col = [to_col(beta_ref[c:c + 1, :]) for c in cs]
    eg_col = [jnp.exp(g) for g in g_col]
    decay = [jnp.where(lower, jnp.exp(jnp.where(lower, gc - gr, 0.0)), 0.0) for gc, gr in zip(g_col, g_row)]
    ktc = [kt[:, r] for r in rows]
    qb = [q_ref[r, :] for r in rows]
    kb = [k_ref[r, :].astype(F32) * b for r, b in zip(rows, b_col)]
    rhs = [jnp.concatenate([v_ref[r, :].astype(F32) * b, k * e], axis=1)
           for r, b, k, e in zip(rows, b_col, kb, eg_col)]
    kk = [_bdot(k, t) for k, t in zip(kb, ktc)]
    qk = [_bdot(q, t) for q, t in zip(qb, ktc)]
    lmat = [jnp.where(strict, k * d, 0.0) for k, d in zip(kk, decay)]
    xt = None
    s = 1
    while s < CHUNK:
        shift = s.bit_length() - 1
        bi = lax.shift_right_logical(ri, shift)
        bj = lax.shift_right_logical(ci, shift)
        pair = (lax.shift_right_logical(bi, 1) == lax.shift_right_logical(bj, 1)) & ((bi & 1) == 1) & ((bj & 1) == 0)
        cblk = [jnp.where(pair, m, 0.0) for m in lmat]
        if xt is None:
            xt = [-c for c in cblk]
        else:
            y = [c + _bdot(x, c) for x, c in zip(xt, cblk)]
            xt = [x - (a + _bdot(a, x)) for x, a in zip(xt, y)]
        s *= 2
    sol = [r + _bdot(x, r) for x, r in zip(xt, rhs)]
    for c in cs:
        u_ref[rows[c], :] = sol[c][:, 0:GDN_DV].astype(BF16)
        w_ref[rows[c], :] = sol[c][:, GDN_DV:GDN_DV + GDN_DK].astype(BF16)
        qe_ref[rows[c], :] = (qb[c].astype(F32) * eg_col[c]).astype(BF16)
        attn_ref[rows[c], :] = jnp.where(lower, qk[c] * decay[c], 0.0).astype(BF16)


def _intra(qn, kn, vn, kt, gate_rows, gate_flat, batch, seq):
    n = qn.shape[0]
    h = GDN_HEADS
    tt = min(2048, seq)
    nc = tt // CHUNK
    per_b = seq // tt
    wide = jax.ShapeDtypeStruct((n, h * LANES), BF16)
    row_spec = pl.BlockSpec((tt, LANES), lambda b, hh, t: (b * per_b + t, hh))
    return pl.pallas_call(
        functools.partial(_intra_body, nc=nc),
        out_shape=(wide, wide, wide,
                   jax.ShapeDtypeStruct((batch, h, LANES, seq), BF16),
                   jax.ShapeDtypeStruct((batch, h, seq, CHUNK), BF16)),
        grid=(batch, h, per_b),
        in_specs=[row_spec, row_spec, row_spec,
                  pl.BlockSpec((None, None, LANES, tt), lambda b, hh, t: (b, hh, 0, t)),
                  pl.BlockSpec((None, None, nc, CHUNK), lambda b, hh, t: (hh, b, t, 0)),
                  pl.BlockSpec((None, None, nc, CHUNK), lambda b, hh, t: (h + hh, b, t, 0)),
                  pl.BlockSpec((h, tt), lambda b, hh, t: (2, b * per_b + t))],
        out_specs=(row_spec, row_spec, row_spec,
                   pl.BlockSpec((None, None, LANES, tt), lambda b, hh, t: (b, hh, 0, t)),
                   pl.BlockSpec((None, None, tt, CHUNK), lambda b, hh, t: (b, hh, t, 0))),
        compiler_params=_params("arbitrary", "arbitrary", "arbitrary"),
        name="gdn_intra",
    )(qn, kn, vn, kt, gate_rows, gate_rows, gate_flat)


def _scan_body(u_ref, w_ref, qe_ref, kdt_ref, attn_ref, egl_ref, z_ref, gn_ref, o_ref, s_scr, *, nct):
    t = pl.program_id(1)

    @pl.when(t == 0)
    def _():
        s_scr[...] = jnp.zeros(s_scr.shape, F32)

    gn = gn_ref[...]
    hs = range(GDN_HEADS)
    cols = [slice(h * LANES, (h + 1) * LANES) for h in hs]
    state = [s_scr[h] for h in hs]
    for c in range(nct):
        rows = slice(c * CHUNK, (c + 1) * CHUNK)
        wq = [jnp.concatenate([w_ref[rows, cl], qe_ref[rows, cl]], axis=0) for cl in cols]
        r = [jnp.dot(a, s.astype(BF16), preferred_element_type=F32) for a, s in zip(wq, state)]
        v_new = [(u_ref[rows, cl].astype(F32) - x[0:CHUNK]).astype(BF16) for cl, x in zip(cols, r)]
        eg = [egl_ref[h, pl.ds(t * nct + c, 1), :][:, 0:1] for h in hs]
        state = [s * e + jnp.dot(kdt_ref[h, :, c * CHUNK:(c + 1) * CHUNK], v, preferred_element_type=F32)
                 for h, s, e, v in zip(hs, state, eg, v_new)]
        o = [x[CHUNK:2 * CHUNK] + jnp.dot(attn_ref[h, rows, :], v, preferred_element_type=F32)
             for h, x, v in zip(hs, r, v_new)]
        for h in hs:
            on = o[h] * lax.rsqrt(jnp.mean(o[h] * o[h], axis=-1, keepdims=True) + EPS) * gn
            o_ref[rows, cols[h]] = (on * _silu(z_ref[rows, cols[h]].astype(F32))).astype(BF16)
    for h in hs:
        s_scr[h] = state[h]


def _scan(u, w, qe, kdt, attn, egl, p_main, gdn_norm, batch, seq, z_blk):
    n = u.shape[0]
    h = GDN_HEADS
    tt = min(512, seq)
    nct = tt // CHUNK
    per_b = seq // tt
    wide_spec = pl.BlockSpec((tt, h * LANES), lambda b, t: (b * per_b + t, 0))
    return pl.pallas_call(
        functools.partial(_scan_body, nct=nct),
        out_shape=jax.ShapeDtypeStruct((n, h * LANES), BF16),
        grid=(batch, per_b),
        in_specs=[wide_spec, wide_spec, wide_spec,
                  pl.BlockSpec((None, h, LANES, tt), lambda b, t: (b, 0, 0, t)),
                  pl.BlockSpec((None, h, tt, CHUNK), lambda b, t: (b, 0, t, 0)),
                  pl.BlockSpec((h, None, seq // CHUNK, CHUNK), lambda b, t: (3, b, 0, 0)),
                  pl.BlockSpec((tt, h * LANES), lambda b, t: (b * per_b + t, z_blk)),
                  pl.BlockSpec((1, LANES), lambda b, t: (0, 0))],
        out_specs=wide_spec,
        scratch_shapes=[pltpu.VMEM((h, GDN_DK, GDN_DV), F32)],
        compiler_params=_params("arbitrary", "arbitrary"),
        name="gdn_scan",
    )(u, w, qe, kdt, attn, egl, p_main, gdn_norm)


def _merge_body(a_ref, b_ref, wa_ref, wb_ref, ga_ref, gb_ref, o_ref):
    ya = jnp.dot(a_ref[...], wa_ref[...], preferred_element_type=F32)
    yb = jnp.dot(b_ref[...], wb_ref[...], preferred_element_type=F32)
    o = _sigmoid(ga_ref[...].astype(F32)) * ya + _sigmoid(gb_ref[...].astype(F32)) * yb
    o_ref[...] = o.astype(o_ref.dtype)


def _merge(o_a, o_b, w_a, w_b, layer, p_main, d):
    n, ka = o_a.shape
    kb = o_b.shape[1]
    tm, tn = 512, d
    nb = d // tn
    return pl.pallas_call(
        _merge_body,
        out_shape=jax.ShapeDtypeStruct((n, d), BF16),
        grid=(n // tm, nb),
        in_specs=[pl.BlockSpec((tm, ka), lambda i, j: (i, 0)),
                  pl.BlockSpec((tm, kb), lambda i, j: (i, 0)),
                  pl.BlockSpec((None, ka, tn), lambda i, j: (layer, 0, j)),
                  pl.BlockSpec((None, kb, tn), lambda i, j: (layer, 0, j)),
                  pl.BlockSpec((tm, tn), lambda i, j: (i, j)),
                  pl.BlockSpec((tm, tn), lambda i, j: (i, nb + j))],
        out_specs=pl.BlockSpec((tm, tn), lambda i, j: (i, j)),
        compiler_params=_params("arbitrary", "arbitrary"),
        name="branch_merge",
    )(o_a, o_b, w_a, w_b, p_main, p_main)


def _resid_body(a_ref, w_ref, x_ref, gt_ref, o_ref):
    y = jnp.dot(a_ref[...], w_ref[...], preferred_element_type=F32)
    o_ref[...] = x_ref[...] + gt_ref[...] * y


def _resid_matmul(a, w, layer, x, gt, seq, tm, tn):
    n, k = a.shape
    d = w.shape[-1]
    tm = min(tm, seq)
    per_b = seq // tm
    return pl.pallas_call(
        _resid_body,
        out_shape=jax.ShapeDtypeStruct((n, d), F32),
        grid=(n // tm, d // tn),
        in_specs=[pl.BlockSpec((tm, k), lambda i, j: (i, 0)),
                  pl.BlockSpec((None, k, tn), lambda i, j: (layer, 0, j)),
                  pl.BlockSpec((tm, tn), lambda i, j: (i, j)),
                  pl.BlockSpec((None, 1, tn), lambda i, j: (i // per_b, 0, j))],
        out_specs=pl.BlockSpec((tm, tn), lambda i, j: (i, j)),
        compiler_params=_params("arbitrary", "arbitrary"),
        name="resid_matmul",
    )(a, w, x, gt)


def _ffn_up_body(x_ref, nw_ref, sc_ref, sh_ref, wg_ref, wu_ref, o_ref, h_even, h_odd, *, n_tiles):
    def norm(fill):
        _norm_slice(x_ref, nw_ref, sc_ref, sh_ref, fill)

    def matmul(cur):
        hb = cur[...]
        gate = jnp.dot(hb, wg_ref[...], preferred_element_type=F32)
        up = jnp.dot(hb, wu_ref[...], preferred_element_type=F32)
        o_ref[...] = (_silu(gate) * up).astype(o_ref.dtype)

    _shifted_dispatch(n_tiles, h_even, h_odd, norm, matmul)


def _ffn_up(x, nw, sc, sh, w_gu, layer, seq):
    n, d = x.shape
    dff = w_gu.shape[-1] // 2
    tm, tn = min(1024, seq), 512
    n_tiles = n // tm
    nb = dff // tn
    in_specs, orow, col = _shifted_specs(tm, d, seq // tm, n_tiles)
    return pl.pallas_call(
        functools.partial(_ffn_up_body, n_tiles=n_tiles),
        out_shape=jax.ShapeDtypeStruct((n, dff), BF16),
        grid=(n_tiles + 1, nb),
        in_specs=in_specs + [pl.BlockSpec((None, d, tn), lambda i, j: (layer, 0, col(i, j))),
                             pl.BlockSpec((None, d, tn), lambda i, j: (layer, 0, nb + col(i, j)))],
        out_specs=pl.BlockSpec((tm, tn), lambda i, j: (orow(i), col(i, j))),
        scratch_shapes=[pltpu.VMEM((tm, d), BF16), pltpu.VMEM((tm, d), BF16)],
        compiler_params=_params("arbitrary", "arbitrary"),
        name="ffn_up",
    )(x, nw, sc, sh, w_gu, w_gu)


def _final_norm_body(x_ref, w_ref, o_ref):
    o_ref[...] = _rms(x_ref[...], w_ref[...])


def _final_norm(x, w):
    n, d = x.shape
    tm = 512
    return pl.pallas_call(
        _final_norm_body,
        out_shape=jax.ShapeDtypeStruct((n, d), F32),
        grid=(n // tm,),
        in_specs=[pl.BlockSpec((tm, d), lambda i: (i, 0)), pl.BlockSpec((1, d), lambda i: (0, 0))],
        out_specs=pl.BlockSpec((tm, d), lambda i: (i, 0)),
        compiler_params=_params("arbitrary"),
        name="final_norm",
    )(x, w)


def _layout_w_in(w_in, d):
    wt = jnp.swapaxes(w_in, 1, 2).astype(BF16)
    o_kpe = Q_LORA + KV_LORA
    o_qkvz = o_kpe + QK_ROPE
    o_ba = o_qkvz + 2 * GDN_QK + 2 * GDN_V
    o_gates = o_ba + 2 * GDN_HEADS
    regions = [(o_gates, 2 * d), (o_qkvz, o_ba - o_qkvz), (0, o_kpe)]
    half = QK_ROPE // 2
    zeros = lambda rows: jnp.zeros((wt.shape[0], rows, wt.shape[2]), wt.dtype)
    w_small = jnp.concatenate([wt[:, o_kpe:o_kpe + half], zeros(LANES // 2 - half),
                               wt[:, o_kpe + half:o_qkvz], zeros(LANES // 2 - half),
                               wt[:, o_ba:o_gates], zeros(LANES - 2 * GDN_HEADS)], axis=1)
    return wt, regions, w_small


def _layout_w_uq(w_uq):
    depth, k, _ = w_uq.shape
    wt = jnp.swapaxes(w_uq, 1, 2).reshape(depth, MLA_HEADS, QK_NOPE + QK_ROPE, k)
    half = QK_ROPE // 2
    z = jnp.zeros((depth, MLA_HEADS, LANES // 2 - half, k), wt.dtype)
    wt = jnp.concatenate([wt[:, :, :QK_NOPE + half], z, wt[:, :, QK_NOPE + half:], z], axis=2)
    return wt.reshape(depth, MLA_HEADS * MLA_QK_PAD, k).astype(BF16)


def _rope_tables(positions):
    inv_freq = 1.0 / (ROPE_THETA ** (jnp.arange(0, QK_ROPE, 2, dtype=F32) / QK_ROPE))
    ang = positions.astype(F32).reshape(-1)[:, None] * inv_freq
    cos, sin = jnp.cos(ang), jnp.sin(ang)
    z = jnp.zeros_like(cos)
    return jnp.concatenate([cos, z, cos, z], axis=1), jnp.concatenate([-sin, z, sin, z], axis=1)


def kernel(x, c, positions, w_ada, b_ada, norm_mix, norm_ffn, w_in, q_a_norm, kv_a_norm, w_uq, w_ukv, w_o_mla,
           conv_w, A_log, dt_bias, gdn_norm, w_o_gdn, w_o, w_gate_up, w_down, final_norm):
    batch, seq, d = x.shape
    depth = w_ada.shape[0]
    n = batch * seq
    h = GDN_HEADS
    cos_t, sin_t = _rope_tables(positions)
    mod = _ada(c, w_ada, b_ada).reshape(depth, batch, 6, 1, d)
    xs = x.reshape(n, d)

    qkv_blk0 = 2 * d // LANES
    z_blk = (2 * d + 2 * GDN_QK + GDN_V) // (h * LANES)
    cq_blk = (2 * d + 2 * GDN_QK + 2 * GDN_V) // Q_LORA
    ckv_blk = cq_blk + 1

    w_in_t, in_regions, w_small = _layout_w_in(w_in, d)
    w_uq_b = _layout_w_uq(w_uq)
    w_ukv_b = w_ukv.astype(BF16)
    w_o_mla_b = w_o_mla.astype(BF16)
    w_o_gdn_b = w_o_gdn.astype(BF16)
    w_o_b = w_o.astype(BF16)
    w_gate_up_b = w_gate_up.astype(BF16)
    w_down_b = w_down.astype(BF16)

    for l in range(depth):
        sh_a, sc_a, gt_a, sh_f, sc_f, gt_f = [mod[l, :, i] for i in range(6)]
        p_main, p_small = _inproj(xs, norm_mix[l].reshape(1, d), sc_a, sh_a, w_in_t, in_regions, w_small, l, seq)

        q, kv, kpe = _mla_proj(p_main, p_small, cos_t, sin_t, q_a_norm[l].reshape(1, -1),
                               kv_a_norm[l].reshape(1, -1), w_uq_b, w_ukv_b, l, cq_blk, ckv_blk)
        o_a = _flash(q.reshape(batch, seq, -1), kv.reshape(batch, seq, -1), kpe.reshape(batch, seq, -1))
        o_a = o_a.reshape(n, -1)

        qn, kn, vn, kt = _conv(p_main, conv_w[l], batch, seq, qkv_blk0)
        gate_flat = _gates(p_small, A_log[l], dt_bias[l])
        gate_rows = gate_flat.reshape(4 * h, batch, seq // CHUNK, CHUNK)
        u, w, qe, kdt, attn = _intra(qn, kn, vn, kt, gate_rows, gate_flat, batch, seq)
        o_b = _scan(u, w, qe, kdt, attn, gate_rows, p_main, gdn_norm[l].reshape(1, -1), batch, seq, z_blk)

        merged = _merge(o_a, o_b, w_o_mla_b, w_o_gdn_b, l, p_main, d)
        xs = _resid_matmul(merged, w_o_b, l, xs, gt_a, seq, 512, d)

        act = _ffn_up(xs, norm_ffn[l].reshape(1, d), sc_f, sh_f, w_gate_up_b, l, seq)
        xs = _resid_matmul(act, w_down_b, l, xs, gt_f, seq, 1024, 512)

    return _final_norm(xs, final_norm.reshape(1, d)).reshape(batch, seq, d)
```

```python
import functools
import math

import jax
import jax.numpy as jnp
from jax import lax
from jax.experimental import pallas as pl
from jax.experimental.pallas import tpu as pltpu

F32 = jnp.float32
BF16 = jnp.bfloat16

MLA_HEADS = 8
QK_NOPE = 128
QK_ROPE = 64
V_HEAD = 128
Q_LORA = 512
KV_LORA = 512
ROPE_THETA = 10000.0
GDN_HEADS = 8
GDN_DK = 128
GDN_DV = 128
CONV_WIDTH = 4
CHUNK = 64
EPS = 1e-6

LANES = 128
BF16_SUBLANES = 16
MLA_QK_PAD = 2 * LANES
VMEM_LIMIT_BYTES = 56 * 1024 * 1024

GDN_QK = GDN_HEADS * GDN_DK
GDN_V = GDN_HEADS * GDN_DV


def _params(*semantics):
    return pltpu.CompilerParams(dimension_semantics=semantics, vmem_limit_bytes=VMEM_LIMIT_BYTES)


def _sigmoid(x):
    return 1.0 / (1.0 + jnp.exp(-x))


def _silu(x):
    return x * _sigmoid(x)


def _rms(x, w):
    return x * lax.rsqrt(jnp.mean(x * x, axis=-1, keepdims=True) + EPS) * w


def _bdot(a, b):
    return jnp.dot(a.astype(BF16), b.astype(BF16), preferred_element_type=F32)


def _ada_body(c_ref, w_ref, b_ref, o_ref):
    c = c_ref[...]
    o_ref[...] = _bdot(_silu(c), w_ref[...]) + b_ref[...]


def _ada(c, w_ada, b_ada):
    depth, d, n6 = w_ada.shape
    b = c.shape[0]
    tn = 1024
    return pl.pallas_call(
        _ada_body,
        out_shape=jax.ShapeDtypeStruct((depth, b, n6), F32),
        grid=(depth, n6 // tn),
        in_specs=[pl.BlockSpec((b, d), lambda l, j: (0, 0)),
                  pl.BlockSpec((None, d, tn), lambda l, j: (l, 0, j)),
                  pl.BlockSpec((None, 1, tn), lambda l, j: (l, 0, j))],
        out_specs=pl.BlockSpec((None, b, tn), lambda l, j: (l, 0, j)),
        compiler_params=_params("arbitrary", "arbitrary"),
        name="ada_mod",
    )(c, w_ada, b_ada.reshape(depth, 1, n6))


NORM_SLICES = 8


def _norm_slice(x_ref, nw_ref, sc_ref, sh_ref, h_ref):
    j = pl.program_id(1)
    rs = x_ref.shape[0] // NORM_SLICES
    r0 = pl.multiple_of(jnp.minimum(j, NORM_SLICES - 1) * rs, rs)
    h = _rms(x_ref[pl.ds(r0, rs), :], nw_ref[...]) * (1.0 + sc_ref[...]) + sh_ref[...]
    h_ref[pl.ds(r0, rs), :] = h.astype(BF16)


def _shifted_dispatch(n_tiles, h_even, h_odd, norm, matmul):
    i = pl.program_id(0)
    bufs = ((h_even, h_odd), (h_odd, h_even))

    @pl.when(i == 0)
    def _():
        norm(h_even)

    for parity, (fill, cur) in enumerate(bufs):
        @pl.when((i % 2 == parity) & (i > 0) & (i < n_tiles))
        def _(fill=fill, cur=cur):
            norm(fill)
            matmul(cur)

    @pl.when(i == n_tiles)
    def _():
        matmul(bufs[n_tiles % 2][1])


def _shifted_specs(tm, d, per_b, n_tiles):
    def xrow(i):
        return jnp.minimum(i, n_tiles - 1)

    in_specs = [pl.BlockSpec((tm, d), lambda i, j: (xrow(i), 0)),
                pl.BlockSpec((1, d), lambda i, j: (0, 0)),
                pl.BlockSpec((None, 1, d), lambda i, j: (xrow(i) // per_b, 0, 0)),
                pl.BlockSpec((None, 1, d), lambda i, j: (xrow(i) // per_b, 0, 0))]

    def orow(i):
        return jnp.maximum(i - 1, 0)

    def col(i, j):
        return jnp.where(i > 0, j, 0)

    return in_specs, orow, col


def _inproj_body(x_ref, nw_ref, sc_ref, sh_ref, w_ref, ws_ref, o_ref, os_ref, h_even, h_odd, *, n_tiles):
    def norm(fill):
        _norm_slice(x_ref, nw_ref, sc_ref, sh_ref, fill)

    nt = (((1,), (1,)), ((), ()))

    def matmul(cur):
        o_ref[...] = lax.dot_general(cur[...], w_ref[0], nt, preferred_element_type=F32).astype(o_ref.dtype)

        @pl.when(pl.program_id(1) == 0)
        def _():
            os_ref[...] = lax.dot_general(cur[...], ws_ref[...], nt, preferred_element_type=F32)

    _shifted_dispatch(n_tiles, h_even, h_odd, norm, matmul)


def _inproj(x, nw, sc, sh, w_t, regions, w_small, layer, seq):
    n, d = x.shape
    nm = sum(rows for _, rows in regions)
    ns = w_small.shape[1]
    tm, tn = min(1024, seq), 1024
    n_tiles = n // tm
    in_specs, orow, col = _shifted_specs(tm, d, seq // tm, n_tiles)

    def w_row(t):
        row, first = 0, 0
        for src, rows in regions:
            assert rows % tn == 0
            assert src % BF16_SUBLANES == 0
            row = jnp.where(t >= first, src + (t - first) * tn, row)
            first += rows // tn
        return pl.multiple_of(row, BF16_SUBLANES)

    return pl.pallas_call(
        functools.partial(_inproj_body, n_tiles=n_tiles),
        out_shape=(jax.ShapeDtypeStruct((n, nm), BF16), jax.ShapeDtypeStruct((n, ns), F32)),
        grid=(n_tiles + 1, nm // tn),
        in_specs=in_specs + [pl.BlockSpec((pl.Element(1), pl.Element(tn), pl.Element(d)),
                                          lambda i, j: (layer, w_row(col(i, j)), 0)),
                             pl.BlockSpec((None, ns, d), lambda i, j: (layer, 0, 0))],
        out_specs=(pl.BlockSpec((tm, tn), lambda i, j: (orow(i), col(i, j))),
                   pl.BlockSpec((tm, ns), lambda i, j: (orow(i), 0))),
        scratch_shapes=[pltpu.VMEM((tm, d), BF16), pltpu.VMEM((tm, d), BF16)],
        compiler_params=_params("arbitrary", "arbitrary"),
        name="in_proj",
    )(x, nw, sc, sh, w_t, w_small)


def _rope128(x, cos, sin):
    return x * cos + pltpu.roll(x, LANES // 2, 1) * sin


def _mla_proj_body(cq_ref, ckv_ref, kpe_ref, cos_ref, sin_ref, qn_ref, kvn_ref, wq_ref, wkv_ref,
                   q_ref, kv_ref, kpeo_ref):
    cos = cos_ref[...]
    sin = sin_ref[...]
    scale = (QK_NOPE + QK_ROPE) ** -0.5 * math.log2(math.e)
    cqn = _rms(cq_ref[...].astype(F32), qn_ref[...]).astype(BF16)
    q = lax.dot_general(cqn, wq_ref[...], (((1,), (1,)), ((), ())), preferred_element_type=F32)
    for h in range(MLA_HEADS):
        lo = h * MLA_QK_PAD
        q_ref[:, lo:lo + LANES] = (q[:, lo:lo + LANES] * scale).astype(BF16)
        pe = _rope128(q[:, lo + LANES:lo + 2 * LANES], cos, sin)
        q_ref[:, lo + LANES:lo + 2 * LANES] = (pe * scale).astype(BF16)
    kv_ref[...] = _bdot(_rms(ckv_ref[...].astype(F32), kvn_ref[...]), wkv_ref[...]).astype(BF16)
    kpeo_ref[...] = _rope128(kpe_ref[...], cos, sin).astype(BF16)


def _mla_proj(p_main, p_small, cos_t, sin_t, qn, kvn, wq, wkv, layer, cq_blk, ckv_blk):
    n = p_main.shape[0]
    tm = 512
    nq, nkv = wq.shape[1], wkv.shape[-1]
    return pl.pallas_call(
        _mla_proj_body,
        out_shape=(jax.ShapeDtypeStruct((n, nq), BF16), jax.ShapeDtypeStruct((n, nkv), BF16),
                   jax.ShapeDtypeStruct((n, LANES), BF16)),
        grid=(n // tm,),
        in_specs=[pl.BlockSpec((tm, Q_LORA), lambda i: (i, cq_blk)),
                  pl.BlockSpec((tm, KV_LORA), lambda i: (i, ckv_blk)),
                  pl.BlockSpec((tm, LANES), lambda i: (i, 0)),
                  pl.BlockSpec((tm, LANES), lambda i: (i, 0)),
                  pl.BlockSpec((tm, LANES), lambda i: (i, 0)),
                  pl.BlockSpec((1, Q_LORA), lambda i: (0, 0)),
                  pl.BlockSpec((1, KV_LORA), lambda i: (0, 0)),
                  pl.BlockSpec((None, nq, Q_LORA), lambda i: (layer, 0, 0)),
                  pl.BlockSpec((None, KV_LORA, nkv), lambda i: (layer, 0, 0))],
        out_specs=(pl.BlockSpec((tm, nq), lambda i: (i, 0)),
                   pl.BlockSpec((tm, nkv), lambda i: (i, 0)),
                   pl.BlockSpec((tm, LANES), lambda i: (i, 0))),
        compiler_params=_params("arbitrary"),
        name="mla_proj",
    )(p_main, p_main, p_small, cos_t, sin_t, qn, kvn, wq, wkv)


def _flash_body(q_ref, kv_ref, kpe_ref, o_ref, *, tq, nq):
    tiles = [slice(i * tq, (i + 1) * tq) for i in range(nq)]
    row = lax.broadcasted_iota(jnp.int32, (tq, tq), 0)
    col = lax.broadcasted_iota(jnp.int32, (tq, tq), 1)
    causal = row >= col

    def scores(r):
        k = jnp.concatenate([kv_ref[tiles[r], 0:QK_NOPE], kpe_ref[tiles[r], :]], axis=1)
        return [lax.dot_general(q_ref[tiles[i], :], k, (((1,), (1,)), ((), ())), preferred_element_type=F32)
                for i in range(r, nq)]

    m = [None] * nq
    l = [None] * nq
    acc = [None] * nq
    s_next = scores(0)
    for r in range(nq):
        s_cur = s_next
        if r + 1 < nq:
            s_next = scores(r + 1)
        v = kv_ref[tiles[r], QK_NOPE:QK_NOPE + V_HEAD]
        p = []
        alpha = []
        for i, s in zip(range(r, nq), s_cur):
            if i == r:
                s = jnp.where(causal, s, -jnp.inf)
            s_max = jnp.max(s, axis=-1, keepdims=True)
            m_new = s_max if r == 0 else jnp.maximum(m[i], s_max)
            e = jnp.exp2(s - m_new)
            e_sum = jnp.sum(e, axis=-1, keepdims=True)
            if r == 0:
                alpha.append(None)
                l[i] = e_sum
            else:
                a = jnp.exp2(m[i] - m_new)
                alpha.append(a)
                l[i] = a * l[i] + e_sum
            m[i] = m_new
            p.append(e.astype(BF16))
        for i, pi, a in zip(range(r, nq), p, alpha):
            pv = jnp.dot(pi, v, preferred_element_type=F32)
            acc[i] = pv if a is None else a * acc[i] + pv
        o_ref[tiles[r], :] = (acc[r] / l[r]).astype(o_ref.dtype)


def _flash(q, kv, kpe):
    b, t, _ = q.shape
    tq = min(512, t)
    return pl.pallas_call(
        functools.partial(_flash_body, tq=tq, nq=t // tq),
        out_shape=jax.ShapeDtypeStruct((b, t, MLA_HEADS * V_HEAD), BF16),
        grid=(b, MLA_HEADS),
        in_specs=[pl.BlockSpec((None, t, MLA_QK_PAD), lambda bi, h: (bi, 0, h)),
                  pl.BlockSpec((None, t, QK_NOPE + V_HEAD), lambda bi, h: (bi, 0, h)),
                  pl.BlockSpec((None, t, LANES), lambda bi, h: (bi, 0, 0))],
        out_specs=pl.BlockSpec((None, t, V_HEAD), lambda bi, h: (bi, 0, h)),
        compiler_params=_params("arbitrary", "arbitrary"),
        name="mla_flash",
    )(q, kv, kpe)


F32_SUBLANES = 8


def _conv_silu(x_ref, w_ref, pad_ref):
    t = x_ref.shape[0]
    u = x_ref[...].astype(F32)
    w = w_ref[...]
    pad_ref[0:F32_SUBLANES, :] = jnp.zeros((F32_SUBLANES, u.shape[1]), F32)
    pad_ref[F32_SUBLANES:F32_SUBLANES + t, :] = u
    y = u * w[CONV_WIDTH - 1:CONV_WIDTH, :]
    for s in range(1, CONV_WIDTH):
        y = y + pad_ref[F32_SUBLANES - s:F32_SUBLANES - s + t, :] * w[CONV_WIDTH - 1 - s:CONV_WIDTH - s, :]
    return _silu(y)


def _l2norm(y):
    return y * lax.rsqrt(jnp.sum(y * y, axis=-1, keepdims=True) + EPS)


def _conv_body(xq_ref, xk_ref, xv_ref, wq_ref, wk_ref, wv_ref, q_ref, k_ref, v_ref, kt_ref,
               pad_q, pad_k, pad_v):
    q_ref[...] = (_l2norm(_conv_silu(xq_ref, wq_ref, pad_q)) * GDN_DK ** -0.5).astype(BF16)
    k = _l2norm(_conv_silu(xk_ref, wk_ref, pad_k))
    k_ref[...] = k.astype(BF16)
    kt_ref[...] = k.T.astype(BF16)
    v_ref[...] = _conv_silu(xv_ref, wv_ref, pad_v).astype(BF16)


def _conv(p_main, conv_w, batch, seq, col_blk0):
    n = p_main.shape[0]
    h = GDN_HEADS
    wide = jax.ShapeDtypeStruct((n, h * LANES), BF16)
    out_spec = pl.BlockSpec((seq, LANES), lambda b, hh: (b, hh))

    def x_spec(kind):
        return pl.BlockSpec((seq, LANES), lambda b, hh: (b, col_blk0 + kind * h + hh))

    def w_spec(kind):
        return pl.BlockSpec((CONV_WIDTH, LANES), lambda b, hh: (0, kind * h + hh))

    return pl.pallas_call(
        _conv_body,
        out_shape=(wide, wide, wide, jax.ShapeDtypeStruct((batch, h, LANES, seq), BF16)),
        grid=(batch, h),
        in_specs=[x_spec(0), x_spec(1), x_spec(2), w_spec(0), w_spec(1), w_spec(2)],
        out_specs=(out_spec, out_spec, out_spec,
                   pl.BlockSpec((None, None, LANES, seq), lambda b, hh: (b, hh, 0, 0))),
        scratch_shapes=[pltpu.VMEM((seq + F32_SUBLANES, LANES), F32)] * 3,
        compiler_params=_params("arbitrary", "arbitrary"),
        name="gdn_conv",
    )(p_main, p_main, p_main, conv_w, conv_w, conv_w)


def _split3(x):
    hi = x.astype(BF16).astype(F32)
    mid = (x - hi).astype(BF16).astype(F32)
    lo = x - hi - mid
    return hi, mid, lo


def _gate_body(s_ref, alog_ref, dtb_ref, o_ref):
    tm = s_ref.shape[0]
    h = GDN_HEADS
    t = s_ref[...].T
    beta = _sigmoid(t[0:h])
    a = t[h:2 * h] + dtb_ref[...]
    softplus = jnp.maximum(a, 0.0) + jnp.log(1.0 + jnp.exp(-jnp.abs(a)))
    g = -jnp.exp(alog_ref[...]) * softplus
    r = lax.broadcasted_iota(jnp.int32, (tm, tm), 0)
    c = lax.broadcasted_iota(jnp.int32, (tm, tm), 1)
    shift = CHUNK.bit_length() - 1
    same = lax.shift_right_logical(r, shift) == lax.shift_right_logical(c, shift)
    cum_m = jnp.where(same & (r <= c), 1.0, 0.0).astype(BF16)
    tot_m = jnp.where(same, 1.0, 0.0).astype(BF16)
    parts = jnp.concatenate(_split3(g), axis=0).astype(BF16)
    cum3 = jnp.dot(parts, cum_m, preferred_element_type=F32)
    tot3 = jnp.dot(parts, tot_m, preferred_element_type=F32)
    gc = cum3[0:h] + cum3[h:2 * h] + cum3[2 * h:3 * h]
    gl = tot3[0:h] + tot3[h:2 * h] + tot3[2 * h:3 * h]
    o_ref[0:h, :] = beta
    o_ref[h:2 * h, :] = gc
    o_ref[2 * h:3 * h, :] = jnp.exp(gl - gc)
    o_ref[3 * h:4 * h, :] = jnp.exp(gl)


def _gates(p_small, a_log, dt_bias):
    n = p_small.shape[0]
    tm = 512
    h = GDN_HEADS
    return pl.pallas_call(
        _gate_body,
        out_shape=jax.ShapeDtypeStruct((4 * h, n), F32),
        grid=(n // tm,),
        in_specs=[pl.BlockSpec((tm, LANES), lambda i: (i, 1)),
                  pl.BlockSpec((h, 1), lambda i: (0, 0)),
                  pl.BlockSpec((h, 1), lambda i: (0, 0))],
        out_specs=pl.BlockSpec((4 * h, tm), lambda i: (0, i)),
        compiler_params=_params("arbitrary"),
        name="gdn_gates",
    )(p_small, a_log.reshape(h, 1), dt_bias.reshape(h, 1))


def _intra_body(q_ref, k_ref, v_ref, kt_ref, beta_ref, gc_ref, dec_ref,
                u_ref, w_ref, qe_ref, kdt_ref, attn_ref, *, nc):
    hh = pl.program_id(1)
    ri = lax.broadcasted_iota(jnp.int32, (CHUNK, CHUNK), 0)
    ci = lax.broadcasted_iota(jnp.int32, (CHUNK, CHUNK), 1)
    eye = ri == ci
    lower = ri >= ci
    strict = ri > ci
    kt = kt_ref[...]
    kdt_ref[...] = (kt.astype(F32) * dec_ref[pl.ds(hh, 1), :]).astype(BF16)

    def to_col(rowvec):
        return jnp.sum(jnp.where(eye, jnp.broadcast_to(rowvec, (CHUNK, CHUNK)), 0.0), axis=1, keepdims=True)

    cs = range(nc)
    rows = [slice(c * CHUNK, (c + 1) * CHUNK) for c in cs]
    g_row = [gc_ref[c:c + 1, :] for c in cs]
    g_col = [to_col(g) for g in g_row]
    b_col = [to_col(beta_ref[c:c + 1, :]) for c in cs]
    eg_col = [jnp.exp(g) for g in g_col]
    decay = [jnp.where(lower, jnp.exp(jnp.where(lower, gc - gr, 0.0)), 0.0) for gc, gr in zip(g_col, g_row)]
    ktc = [kt[:, r] for r in rows]
    qb = [q_ref[r, :] for r in rows]
    kb = [k_ref[r, :].astype(F32) * b for r, b in zip(rows, b_col)]
    rhs = [jnp.concatenate([v_ref[r, :].astype(F32) * b, k * e], axis=1)
           for r, b, k, e in zip(rows, b_col, kb, eg_col)]
    kk = [_bdot(k, t) for k, t in zip(kb, ktc)]
    qk = [_bdot(q, t) for q, t in zip(qb, ktc)]
    lmat = [jnp.where(strict, k * d, 0.0) for k, d in zip(kk, decay)]
    xt = None
    s = 1
    while s < CHUNK:
        shift = s.bit_length() - 1
        bi = lax.shift_right_logical(ri, shift)
        bj = lax.shift_right_logical(ci, shift)
        pair = (lax.shift_right_logical(bi, 1) == lax.shift_right_logical(bj, 1)) & ((bi & 1) == 1) & ((bj & 1) == 0)
        cblk = [jnp.where(pair, m, 0.0) for m in lmat]
        if xt is None:
            xt = [-c for c in cblk]
        else:
            y = [c + _bdot(x, c) for x, c in zip(xt, cblk)]
            xt = [x - (a + _bdot(a, x)) for x, a in zip(xt, y)]
        s *= 2
    sol = [r + _bdot(x, r) for x, r in zip(xt, rhs)]
    for c in cs:
        u_ref[rows[c], :] = sol[c][:, 0:GDN_DV].astype(BF16)
        w_ref[rows[c], :] = sol[c][:, GDN_DV:GDN_DV + GDN_DK].astype(BF16)
        qe_ref[rows[c], :] = (qb[c].astype(F32) * eg_col[c]).astype(BF16)
        attn_ref[rows[c], :] = jnp.where(lower, qk[c] * decay[c], 0.0).astype(BF16)


def _intra(qn, kn, vn, kt, gate_rows, gate_flat, batch, seq):
    n = qn.shape[0]
    h = GDN_HEADS
    tt = min(2048, seq)
    nc = tt // CHUNK
    per_b = seq // tt
    wide = jax.ShapeDtypeStruct((n, h * LANES), BF16)
    row_spec = pl.BlockSpec((tt, LANES), lambda b, hh, t: (b * per_b + t, hh))
    return pl.pallas_call(
        functools.partial(_intra_body, nc=nc),
        out_shape=(wide, wide, wide,
                   jax.ShapeDtypeStruct((batch, h, LANES, seq), BF16),
                   jax.ShapeDtypeStruct((batch, h, seq, CHUNK), BF16)),
        grid=(batch, h, per_b),
        in_specs=[row_spec, row_spec, row_spec,
                  pl.BlockSpec((None, None, LANES, tt), lambda b, hh, t: (b, hh, 0, t)),
                  pl.BlockSpec((None, None, nc, CHUNK), lambda b, hh, t: (hh, b, t, 0)),
                  pl.BlockSpec((None, None, nc, CHUNK), lambda b, hh, t: (h + hh, b, t, 0)),
                  pl.BlockSpec((h, tt), lambda b, hh, t: (2, b * per_b + t))],
        out_specs=(row_spec, row_spec, row_spec,
                   pl.BlockSpec((None, None, LANES, tt), lambda b, hh, t: (b, hh, 0, t)),
                   pl.BlockSpec((None, None, tt, CHUNK), lambda b, hh, t: (b, hh, t, 0))),
        compiler_params=_params("arbitrary", "arbitrary", "arbitrary"),
        name="gdn_intra",
    )(qn, kn, vn, kt, gate_rows, gate_rows, gate_flat)


def _scan_body(u_ref, w_ref, qe_ref, kdt_ref, attn_ref, egl_ref, z_ref, gn_ref, o_ref, s_scr, *, nct):
    t = pl.program_id(1)

    @pl.when(t == 0)
    def _():
        s_scr[...] = jnp.zeros(s_scr.shape, F32)

    gn = gn_ref[...]
    hs = range(GDN_HEADS)
    cols = [slice(h * LANES, (h + 1) * LANES) for h in hs]
    state = [s_scr[h] for h in hs]
    for c in range(nct):
        rows = slice(c * CHUNK, (c + 1) * CHUNK)
        wq = [jnp.concatenate([w_ref[rows, cl], qe_ref[rows, cl]], axis=0) for cl in cols]
        r = [jnp.dot(a, s.astype(BF16), preferred_element_type=F32) for a, s in zip(wq, state)]
        v_new = [(u_ref[rows, cl].astype(F32) - x[0:CHUNK]).astype(BF16) for cl, x in zip(cols, r)]
        eg = [egl_ref[h, pl.ds(t * nct + c, 1), :][:, 0:1] for h in hs]
        state = [s * e + jnp.dot(kdt_ref[h, :, c * CHUNK:(c + 1) * CHUNK], v, preferred_element_type=F32)
                 for h, s, e, v in zip(hs, state, eg, v_new)]
        o = [x[CHUNK:2 * CHUNK] + jnp.dot(attn_ref[h, rows, :], v, preferred_element_type=F32)
             for h, x, v in zip(hs, r, v_new)]
        for h in hs:
            on = o[h] * lax.rsqrt(jnp.mean(o[h] * o[h], axis=-1, keepdims=True) + EPS) * gn
            o_ref[rows, cols[h]] = (on * _silu(z_ref[rows, cols[h]].astype(F32))).astype(BF16)
    for h in hs:
        s_scr[h] = state[h]


def _scan(u, w, qe, kdt, attn, egl, p_main, gdn_norm, batch, seq, z_blk):
    n = u.shape[0]
    h = GDN_HEADS
    tt = min(512, seq)
    nct = tt // CHUNK
    per_b = seq // tt
    wide_spec = pl.BlockSpec((tt, h * LANES), lambda b, t: (b * per_b + t, 0))
    return pl.pallas_call(
        functools.partial(_scan_body, nct=nct),
        out_shape=jax.ShapeDtypeStruct((n, h * LANES), BF16),
        grid=(batch, per_b),
        in_specs=[wide_spec, wide_spec, wide_spec,
                  pl.BlockSpec((None, h, LANES, tt), lambda b, t: (b, 0, 0, t)),
                  pl.BlockSpec((None, h, tt, CHUNK), lambda b, t: (b, 0, t, 0)),
                  pl.BlockSpec((h, None, seq // CHUNK, CHUNK), lambda b, t: (3, b, 0, 0)),
                  pl.BlockSpec((tt, h * LANES), lambda b, t: (b * per_b + t, z_blk)),
                  pl.BlockSpec((1, LANES), lambda b, t: (0, 0))],
        out_specs=wide_spec,
        scratch_shapes=[pltpu.VMEM((h, GDN_DK, GDN_DV), F32)],
        compiler_params=_params("arbitrary", "arbitrary"),
        name="gdn_scan",
    )(u, w, qe, kdt, attn, egl, p_main, gdn_norm)


def _merge_body(a_ref, b_ref, wa_ref, wb_ref, ga_ref, gb_ref, o_ref):
    ya = jnp.dot(a_ref[...], wa_ref[...], preferred_element_type=F32)
    yb = jnp.dot(b_ref[...], wb_ref[...], preferred_element_type=F32)
    o = _sigmoid(ga_ref[...].astype(F32)) * ya + _sigmoid(gb_ref[...].astype(F32)) * yb
    o_ref[...] = o.astype(o_ref.dtype)


def _merge(o_a, o_b, w_a, w_b, layer, p_main, d):
    n, ka = o_a.shape
    kb = o_b.shape[1]
    tm, tn = 512, d
    nb = d // tn
    return pl.pallas_call(
        _merge_body,
        out_shape=jax.ShapeDtypeStruct((n, d), BF16),
        grid=(n // tm, nb),
        in_specs=[pl.BlockSpec((tm, ka), lambda i, j: (i, 0)),
                  pl.BlockSpec((tm, kb), lambda i, j: (i, 0)),
                  pl.BlockSpec((None, ka, tn), lambda i, j: (layer, 0, j)),
                  pl.BlockSpec((None, kb, tn), lambda i, j: (layer, 0, j)),
                  pl.BlockSpec((tm, tn), lambda i, j: (i, j)),
                  pl.BlockSpec((tm, tn), lambda i, j: (i, nb + j))],
        out_specs=pl.BlockSpec((tm, tn), lambda i, j: (i, j)),
        compiler_params=_params("arbitrary", "arbitrary"),
        name="branch_merge",
    )(o_a, o_b, w_a, w_b, p_main, p_main)


def _resid_body(a_ref, w_ref, x_ref, gt_ref, o_ref):
    y = jnp.dot(a_ref[...], w_ref[...], preferred_element_type=F32)
    o_ref[...] = x_ref[...] + gt_ref[...] * y


def _resid_matmul(a, w, layer, x, gt, seq, tm, tn):
    n, k = a.shape
    d = w.shape[-1]
    tm = min(tm, seq)
    per_b = seq // tm
    return pl.pallas_call(
        _resid_body,
        out_shape=jax.ShapeDtypeStruct((n, d), F32),
        grid=(n // tm, d // tn),
        in_specs=[pl.BlockSpec((tm, k), lambda i, j: (i, 0)),
                  pl.BlockSpec((None, k, tn), lambda i, j: (layer, 0, j)),
                  pl.BlockSpec((tm, tn), lambda i, j: (i, j)),
                  pl.BlockSpec((None, 1, tn), lambda i, j: (i // per_b, 0, j))],
        out_specs=pl.BlockSpec((tm, tn), lambda i, j: (i, j)),
        compiler_params=_params("arbitrary", "arbitrary"),
        name="resid_matmul",
    )(a, w, x, gt)


def _ffn_up_body(x_ref, nw_ref, sc_ref, sh_ref, wg_ref, wu_ref, o_ref, h_even, h_odd, *, n_tiles):
    def norm(fill):
        _norm_slice(x_ref, nw_ref, sc_ref, sh_ref, fill)

    def matmul(cur):
        hb = cur[...]
        gate = jnp.dot(hb, wg_ref[...], preferred_element_type=F32)
        up = jnp.dot(hb, wu_ref[...], preferred_element_type=F32)
        o_ref[...] = (_silu(gate) * up).astype(o_ref.dtype)

    _shifted_dispatch(n_tiles, h_even, h_odd, norm, matmul)


def _ffn_up(x, nw, sc, sh, w_gu, layer, seq):
    n, d = x.shape
    dff = w_gu.shape[-1] // 2
    tm, tn = min(1024, seq), 512
    n_tiles = n // tm
    nb = dff // tn
    in_specs, orow, col = _shifted_specs(tm, d, seq // tm, n_tiles)
    return pl.pallas_call(
        functools.partial(_ffn_up_body, n_tiles=n_tiles),
        out_shape=jax.ShapeDtypeStruct((n, dff), BF16),
        grid=(n_tiles + 1, nb),
        in_specs=in_specs + [pl.BlockSpec((None, d, tn), lambda i, j: (layer, 0, col(i, j))),
                             pl.BlockSpec((None, d, tn), lambda i, j: (layer, 0, nb + col(i, j)))],
        out_specs=pl.BlockSpec((tm, tn), lambda i, j: (orow(i), col(i, j))),
        scratch_shapes=[pltpu.VMEM((tm, d), BF16), pltpu.VMEM((tm, d), BF16)],
        compiler_params=_params("arbitrary", "arbitrary"),
        name="ffn_up",
    )(x, nw, sc, sh, w_gu, w_gu)


def _final_norm_body(x_ref, w_ref, o_ref):
    o_ref[...] = _rms(x_ref[...], w_ref[...])


def _final_norm(x, w):
    n, d = x.shape
    tm = 512
    return pl.pallas_call(
        _final_norm_body,
        out_shape=jax.ShapeDtypeStruct((n, d), F32),
        grid=(n // tm,),
        in_specs=[pl.BlockSpec((tm, d), lambda i: (i, 0)), pl.BlockSpec((1, d), lambda i: (0, 0))],
        out_specs=pl.BlockSpec((tm, d), lambda i: (i, 0)),
        compiler_params=_params("arbitrary"),
        name="final_norm",
    )(x, w)


def _layout_w_in(w_in, d):
    wt = jnp.swapaxes(w_in, 1, 2).astype(BF16)
    o_kpe = Q_LORA + KV_LORA
    o_qkvz = o_kpe + QK_ROPE
    o_ba = o_qkvz + 2 * GDN_QK + 2 * GDN_V
    o_gates = o_ba + 2 * GDN_HEADS
    regions = [(o_gates, 2 * d), (o_qkvz, o_ba - o_qkvz), (0, o_kpe)]
    half = QK_ROPE // 2
    zeros = lambda rows: jnp.zeros((wt.shape[0], rows, wt.shape[2]), wt.dtype)
    w_small = jnp.concatenate([wt[:, o_kpe:o_kpe + half], zeros(LANES // 2 - half),
                               wt[:, o_kpe + half:o_qkvz], zeros(LANES // 2 - half),
                               wt[:, o_ba:o_gates], zeros(LANES - 2 * GDN_HEADS)], axis=1)
    return wt, regions, w_small


def _layout_w_uq(w_uq):
    depth, k, _ = w_uq.shape
    wt = jnp.swapaxes(w_uq, 1, 2).reshape(depth, MLA_HEADS, QK_NOPE + QK_ROPE, k)
    half = QK_ROPE // 2
    z = jnp.zeros((depth, MLA_HEADS, LANES // 2 - half, k), wt.dtype)
    wt = jnp.concatenate([wt[:, :, :QK_NOPE + half], z, wt[:, :, QK_NOPE + half:], z], axis=2)
    return wt.reshape(depth, MLA_HEADS * MLA_QK_PAD, k).astype(BF16)


def _rope_tables(positions):
    inv_freq = 1.0 / (ROPE_THETA ** (jnp.arange(0, QK_ROPE, 2, dtype=F32) / QK_ROPE))
    ang = positions.astype(F32).reshape(-1)[:, None] * inv_freq
    cos, sin = jnp.cos(ang), jnp.sin(ang)
    z = jnp.zeros_like(cos)
    return jnp.concatenate([cos, z, cos, z], axis=1), jnp.concatenate([-sin, z, sin, z], axis=1)


def kernel(x, c, positions, w_ada, b_ada, norm_mix, norm_ffn, w_in, q_a_norm, kv_a_norm, w_uq, w_ukv, w_o_mla,
           conv_w, A_log, dt_bias, gdn_norm, w_o_gdn, w_o, w_gate_up, w_down, final_norm):
    batch, seq, d = x.shape
    depth = w_ada.shape[0]
    n = batch * seq
    h = GDN_HEADS
    cos_t, sin_t = _rope_tables(positions)
    mod = _ada(c, w_ada, b_ada).reshape(depth, batch, 6, 1, d)
    xs = x.reshape(n, d)

    qkv_blk0 = 2 * d // LANES
    z_blk = (2 * d + 2 * GDN_QK + GDN_V) // (h * LANES)
    cq_blk = (2 * d + 2 * GDN_QK + 2 * GDN_V) // Q_LORA
    ckv_blk = cq_blk + 1

    w_in_t, in_regions, w_small = _layout_w_in(w_in, d)
    w_uq_b = _layout_w_uq(w_uq)
    w_ukv_b = w_ukv.astype(BF16)
    w_o_mla_b = w_o_mla.astype(BF16)
    w_o_gdn_b = w_o_gdn.astype(BF16)
    w_o_b = w_o.astype(BF16)
    w_gate_up_b = w_gate_up.astype(BF16)
    w_down_b = w_down.astype(BF16)

    for l in range(depth):
        sh_a, sc_a, gt_a, sh_f, sc_f, gt_f = [mod[l, :, i] for i in range(6)]
        p_main, p_small = _inproj(xs, norm_mix[l].reshape(1, d), sc_a, sh_a, w_in_t, in_regions, w_small, l, seq)

        q, kv, kpe = _mla_proj(p_main, p_small, cos_t, sin_t, q_a_norm[l].reshape(1, -1),
                               kv_a_norm[l].reshape(1, -1), w_uq_b, w_ukv_b, l, cq_blk, ckv_blk)
        o_a = _flash(q.reshape(batch, seq, -1), kv.reshape(batch, seq, -1), kpe.reshape(batch, seq, -1))
        o_a = o_a.reshape(n, -1)

        qn, kn, vn, kt = _conv(p_main, conv_w[l], batch, seq, qkv_blk0)
        gate_flat = _gates(p_small, A_log[l], dt_bias[l])
        gate_rows = gate_flat.reshape(4 * h, batch, seq // CHUNK, CHUNK)
        u, w, qe, kdt, attn = _intra(qn, kn, vn, kt, gate_rows, gate_flat, batch, seq)
        o_b = _scan(u, w, qe, kdt, attn, gate_rows, p_main, gdn_norm[l].reshape(1, -1), batch, seq, z_blk)

        merged = _merge(o_a, o_b, w_o_mla_b, w_o_gdn_b, l, p_main, d)
        xs = _resid_matmul(merged, w_o_b, l, xs, gt_a, seq, 512, d)

        act = _ffn_up(xs, norm_ffn[l].reshape(1, d), sc_f, sh_f, w_gate_up_b, l, seq)
        xs = _resid_matmul(act, w_down_b, l, xs, gt_f, seq, 1024, 512)

    return _final_norm(xs, final_norm.reshape(1, d)).reshape(batch, seq, d)
```

```python
import functools
import math

import jax
import jax.numpy as jnp
from jax import lax
from jax.experimental import pallas as pl
from jax.experimental.pallas import tpu as pltpu

F32 = jnp.float32
BF16 = jnp.bfloat16

MLA_HEADS = 8
QK_NOPE = 128
QK_ROPE = 64
V_HEAD = 128
Q_LORA = 512
KV_LORA = 512
ROPE_THETA = 10000.0
GDN_HEADS = 8
GDN_DK = 128
GDN_DV = 128
CONV_WIDTH = 4
CHUNK = 64
EPS = 1e-6

LANES = 128
BF16_SUBLANES = 16
MLA_QK_PAD = 2 * LANES
VMEM_LIMIT_BYTES = 56 * 1024 * 1024

GDN_QK = GDN_HEADS * GDN_DK
GDN_V = GDN_HEADS * GDN_DV


def _params(*semantics):
    return pltpu.CompilerParams(dimension_semantics=semantics, vmem_limit_bytes=VMEM_LIMIT_BYTES)


def _sigmoid(x):
    return 1.0 / (1.0 + jnp.exp(-x))


def _silu(x):
    return x * _sigmoid(x)


def _rms(x, w):
    return x * lax.rsqrt(jnp.mean(x * x, axis=-1, keepdims=True) + EPS) * w


def _bdot(a, b):
    return jnp.dot(a.astype(BF16), b.astype(BF16), preferred_element_type=F32)


def _ada_body(c_ref, w_ref, b_ref, o_ref):
    c = c_ref[...]
    o_ref[...] = _bdot(_silu(c), w_ref[...]) + b_ref[...]


def _ada(c, w_ada, b_ada):
    depth, d, n6 = w_ada.shape
    b = c.shape[0]
    tn = 1024
    return pl.pallas_call(
        _ada_body,
        out_shape=jax.ShapeDtypeStruct((depth, b, n6), F32),
        grid=(depth, n6 // tn),
        in_specs=[pl.BlockSpec((b, d), lambda l, j: (0, 0)),
                  pl.BlockSpec((None, d, tn), lambda l, j: (l, 0, j)),
                  pl.BlockSpec((None, 1, tn), lambda l, j: (l, 0, j))],
        out_specs=pl.BlockSpec((None, b, tn), lambda l, j: (l, 0, j)),
        compiler_params=_params("arbitrary", "arbitrary"),
        name="ada_mod",
    )(c, w_ada, b_ada.reshape(depth, 1, n6))


NORM_SLICES = 8


def _norm_slice(x_ref, nw_ref, sc_ref, sh_ref, h_ref):
    j = pl.program_id(1)
    rs = x_ref.shape[0] // NORM_SLICES
    r0 = pl.multiple_of(jnp.minimum(j, NORM_SLICES - 1) * rs, rs)
    h = _rms(x_ref[pl.ds(r0, rs), :], nw_ref[...]) * (1.0 + sc_ref[...]) + sh_ref[...]
    h_ref[pl.ds(r0, rs), :] = h.astype(BF16)


def _shifted_dispatch(n_tiles, h_even, h_odd, norm, matmul):
    i = pl.program_id(0)
    bufs = ((h_even, h_odd), (h_odd, h_even))

    @pl.when(i == 0)
    def _():
        norm(h_even)

    for parity, (fill, cur) in enumerate(bufs):
        @pl.when((i % 2 == parity) & (i > 0) & (i < n_tiles))
        def _(fill=fill, cur=cur):
            norm(fill)
            matmul(cur)

    @pl.when(i == n_tiles)
    def _():
        matmul(bufs[n_tiles % 2][1])


def _shifted_specs(tm, d, per_b, n_tiles):
    def xrow(i):
        return jnp.minimum(i, n_tiles - 1)

    in_specs = [pl.BlockSpec((tm, d), lambda i, j: (xrow(i), 0)),
                pl.BlockSpec((1, d), lambda i, j: (0, 0)),
                pl.BlockSpec((None, 1, d), lambda i, j: (xrow(i) // per_b, 0, 0)),
                pl.BlockSpec((None, 1, d), lambda i, j: (xrow(i) // per_b, 0, 0))]

    def orow(i):
        return jnp.maximum(i - 1, 0)

    def col(i, j):
        return jnp.where(i > 0, j, 0)

    return in_specs, orow, col


def _inproj_body(x_ref, nw_ref, sc_ref, sh_ref, w_ref, ws_ref, o_ref, os_ref, h_even, h_odd, *, n_tiles):
    def norm(fill):
        _norm_slice(x_ref, nw_ref, sc_ref, sh_ref, fill)

    nt = (((1,), (1,)), ((), ()))

    def matmul(cur):
        o_ref[...] = lax.dot_general(cur[...], w_ref[0], nt, preferred_element_type=F32).astype(o_ref.dtype)

        @pl.when(pl.program_id(1) == 0)
        def _():
            os_ref[...] = lax.dot_general(cur[...], ws_ref[...], nt, preferred_element_type=F32)

    _shifted_dispatch(n_tiles, h_even, h_odd, norm, matmul)


def _inproj(x, nw, sc, sh, w_t, regions, w_small, layer, seq):
    n, d = x.shape
    nm = sum(rows for _, rows in regions)
    ns = w_small.shape[1]
    tm, tn = min(1024, seq), 1024
    n_tiles = n // tm
    in_specs, orow, col = _shifted_specs(tm, d, seq // tm, n_tiles)

    def w_row(t):
        row, first = 0, 0
        for src, rows in regions:
            assert rows % tn == 0
            assert src % BF16_SUBLANES == 0
            row = jnp.where(t >= first, src + (t - first) * tn, row)
            first += rows // tn
        return pl.multiple_of(row, BF16_SUBLANES)

    return pl.pallas_call(
        functools.partial(_inproj_body, n_tiles=n_tiles),
        out_shape=(jax.ShapeDtypeStruct((n, nm), BF16), jax.ShapeDtypeStruct((n, ns), F32)),
        grid=(n_tiles + 1, nm // tn),
        in_specs=in_specs + [pl.BlockSpec((pl.Element(1), pl.Element(tn), pl.Element(d)),
                                          lambda i, j: (layer, w_row(col(i, j)), 0)),
                             pl.BlockSpec((None, ns, d), lambda i, j: (layer, 0, 0))],
        out_specs=(pl.BlockSpec((tm, tn), lambda i, j: (orow(i), col(i, j))),
                   pl.BlockSpec((tm, ns), lambda i, j: (orow(i), 0))),
        scratch_shapes=[pltpu.VMEM((tm, d), BF16), pltpu.VMEM((tm, d), BF16)],
        compiler_params=_params("arbitrary", "arbitrary"),
        name="in_proj",
    )(x, nw, sc, sh, w_t, w_small)


def _rope128(x, cos, sin):
    return x * cos + pltpu.roll(x, LANES // 2, 1) * sin


def _mla_proj_body(cq_ref, ckv_ref, kpe_ref, cos_ref, sin_ref, qn_ref, kvn_ref, wq_ref, wkv_ref,
                   q_ref, kv_ref, kpeo_ref):
    cos = cos_ref[...]
    sin = sin_ref[...]
    scale = (QK_NOPE + QK_ROPE) ** -0.5 * math.log2(math.e)
    cqn = _rms(cq_ref[...].astype(F32), qn_ref[...]).astype(BF16)
    q = lax.dot_general(cqn, wq_ref[...], (((1,), (1,)), ((), ())), preferred_element_type=F32)
    for h in range(MLA_HEADS):
        lo = h * MLA_QK_PAD
        q_ref[:, lo:lo + LANES] = (q[:, lo:lo + LANES] * scale).astype(BF16)
        pe = _rope128(q[:, lo + LANES:lo + 2 * LANES], cos, sin)
        q_ref[:, lo + LANES:lo + 2 * LANES] = (pe * scale).astype(BF16)
    kv_ref[...] = _bdot(_rms(ckv_ref[...].astype(F32), kvn_ref[...]), wkv_ref[...]).astype(BF16)
    kpeo_ref[...] = _rope128(kpe_ref[...], cos, sin).astype(BF16)


def _mla_proj(p_main, p_small, cos_t, sin_t, qn, kvn, wq, wkv, layer, cq_blk, ckv_blk):
    n = p_main.shape[0]
    tm = 1024
    nq, nkv = wq.shape[1], wkv.shape[-1]
    return pl.pallas_call(
        _mla_proj_body,
        out_shape=(jax.ShapeDtypeStruct((n, nq), BF16), jax.ShapeDtypeStruct((n, nkv), BF16),
                   jax.ShapeDtypeStruct((n, LANES), BF16)),
        grid=(n // tm,),
        in_specs=[pl.BlockSpec((tm, Q_LORA), lambda i: (i, cq_blk)),
                  pl.BlockSpec((tm, KV_LORA), lambda i: (i, ckv_blk)),
                  pl.BlockSpec((tm, LANES), lambda i: (i, 0)),
                  pl.BlockSpec((tm, LANES), lambda i: (i, 0)),
                  pl.BlockSpec((tm, LANES), lambda i: (i, 0)),
                  pl.BlockSpec((1, Q_LORA), lambda i: (0, 0)),
                  pl.BlockSpec((1, KV_LORA), lambda i: (0, 0)),
                  pl.BlockSpec((None, nq, Q_LORA), lambda i: (layer, 0, 0)),
                  pl.BlockSpec((None, KV_LORA, nkv), lambda i: (layer, 0, 0))],
        out_specs=(pl.BlockSpec((tm, nq), lambda i: (i, 0)),
                   pl.BlockSpec((tm, nkv), lambda i: (i, 0)),
                   pl.BlockSpec((tm, LANES), lambda i: (i, 0))),
        compiler_params=_params("arbitrary"),
        name="mla_proj",
    )(p_main, p_main, p_small, cos_t, sin_t, qn, kvn, wq, wkv)


def _flash_body(q_ref, kv_ref, kpe_ref, o_ref, *, tq, nq):
    tiles = [slice(i * tq, (i + 1) * tq) for i in range(nq)]
    row = lax.broadcasted_iota(jnp.int32, (tq, tq), 0)
    col = lax.broadcasted_iota(jnp.int32, (tq, tq), 1)
    causal = row >= col

    def scores(r):
        k = jnp.concatenate([kv_ref[tiles[r], 0:QK_NOPE], kpe_ref[tiles[r], :]], axis=1)
        return [lax.dot_general(q_ref[tiles[i], :], k, (((1,), (1,)), ((), ())), preferred_element_type=F32)
                for i in range(r, nq)]

    m = [None] * nq
    l = [None] * nq
    acc = [None] * nq
    s_next = scores(0)
    for r in range(nq):
        s_cur = s_next
        if r + 1 < nq:
            s_next = scores(r + 1)
        v = kv_ref[tiles[r], QK_NOPE:QK_NOPE + V_HEAD]
        p = []
        alpha = []
        for i, s in zip(range(r, nq), s_cur):
            if i == r:
                s = jnp.where(causal, s, -jnp.inf)
            s_max = jnp.max(s, axis=-1, keepdims=True)
            m_new = s_max if r == 0 else jnp.maximum(m[i], s_max)
            e = jnp.exp2(s - m_new)
            e_sum = jnp.sum(e, axis=-1, keepdims=True)
            if r == 0:
                alpha.append(None)
                l[i] = e_sum
            else:
                a = jnp.exp2(m[i] - m_new)
                alpha.append(a)
                l[i] = a * l[i] + e_sum
            m[i] = m_new
            p.append(e.astype(BF16))
        for i, pi, a in zip(range(r, nq), p, alpha):
            pv = jnp.dot(pi, v, preferred_element_type=F32)
            acc[i] = pv if a is None else a * acc[i] + pv
        o_ref[tiles[r], :] = (acc[r] / l[r]).astype(o_ref.dtype)


def _flash(q, kv, kpe):
    b, t, _ = q.shape
    tq = min(512, t)
    return pl.pallas_call(
        functools.partial(_flash_body, tq=tq, nq=t // tq),
        out_shape=jax.ShapeDtypeStruct((b, t, MLA_HEADS * V_HEAD), BF16),
        grid=(b, MLA_HEADS),
        in_specs=[pl.BlockSpec((None, t, MLA_QK_PAD), lambda bi, h: (bi, 0, h)),
                  pl.BlockSpec((None, t, QK_NOPE + V_HEAD), lambda bi, h: (bi, 0, h)),
                  pl.BlockSpec((None, t, LANES), lambda bi, h: (bi, 0, 0))],
        out_specs=pl.BlockSpec((None, t, V_HEAD), lambda bi, h: (bi, 0, h)),
        compiler_params=_params("arbitrary", "arbitrary"),
        name="mla_flash",
    )(q, kv, kpe)


F32_SUBLANES = 8


def _conv_silu(x_ref, w_ref, pad_ref):
    t = x_ref.shape[0]
    u = x_ref[...].astype(F32)
    w = w_ref[...]
    pad_ref[0:F32_SUBLANES, :] = jnp.zeros((F32_SUBLANES, u.shape[1]), F32)
    pad_ref[F32_SUBLANES:F32_SUBLANES + t, :] = u
    y = u * w[CONV_WIDTH - 1:CONV_WIDTH, :]
    for s in range(1, CONV_WIDTH):
        y = y + pad_ref[F32_SUBLANES - s:F32_SUBLANES - s + t, :] * w[CONV_WIDTH - 1 - s:CONV_WIDTH - s, :]
    return _silu(y)


def _l2norm(y):
    return y * lax.rsqrt(jnp.sum(y * y, axis=-1, keepdims=True) + EPS)


def _conv_body(xq_ref, xk_ref, xv_ref, wq_ref, wk_ref, wv_ref, q_ref, k_ref, v_ref, kt_ref,
               pad_q, pad_k, pad_v):
    q_ref[...] = (_l2norm(_conv_silu(xq_ref, wq_ref, pad_q)) * GDN_DK ** -0.5).astype(BF16)
    k = _l2norm(_conv_silu(xk_ref, wk_ref, pad_k))
    k_ref[...] = k.astype(BF16)
    kt_ref[...] = k.T.astype(BF16)
    v_ref[...] = _conv_silu(xv_ref, wv_ref, pad_v).astype(BF16)


def _conv(p_main, conv_w, batch, seq, col_blk0):
    n = p_main.shape[0]
    h = GDN_HEADS
    wide = jax.ShapeDtypeStruct((n, h * LANES), BF16)
    out_spec = pl.BlockSpec((seq, LANES), lambda b, hh: (b, hh))

    def x_spec(kind):
        return pl.BlockSpec((seq, LANES), lambda b, hh: (b, col_blk0 + kind * h + hh))

    def w_spec(kind):
        return pl.BlockSpec((CONV_WIDTH, LANES), lambda b, hh: (0, kind * h + hh))

    return pl.pallas_call(
        _conv_body,
        out_shape=(wide, wide, wide, jax.ShapeDtypeStruct((batch, h, LANES, seq), BF16)),
        grid=(batch, h),
        in_specs=[x_spec(0), x_spec(1), x_spec(2), w_spec(0), w_spec(1), w_spec(2)],
        out_specs=(out_spec, out_spec, out_spec,
                   pl.BlockSpec((None, None, LANES, seq), lambda b, hh: (b, hh, 0, 0))),
        scratch_shapes=[pltpu.VMEM((seq + F32_SUBLANES, LANES), F32)] * 3,
        compiler_params=_params("arbitrary", "arbitrary"),
        name="gdn_conv",
    )(p_main, p_main, p_main, conv_w, conv_w, conv_w)


def _split3(x):
    hi = x.astype(BF16).astype(F32)
    mid = (x - hi).astype(BF16).astype(F32)
    lo = x - hi - mid
    return hi, mid, lo


def _gate_body(s_ref, alog_ref, dtb_ref, o_ref):
    tm = s_ref.shape[0]
    h = GDN_HEADS
    t = s_ref[...].T
    beta = _sigmoid(t[0:h])
    a = t[h:2 * h] + dtb_ref[...]
    softplus = jnp.maximum(a, 0.0) + jnp.log(1.0 + jnp.exp(-jnp.abs(a)))
    g = -jnp.exp(alog_ref[...]) * softplus
    r = lax.broadcasted_iota(jnp.int32, (tm, tm), 0)
    c = lax.broadcasted_iota(jnp.int32, (tm, tm), 1)
    shift = CHUNK.bit_length() - 1
    same = lax.shift_right_logical(r, shift) == lax.shift_right_logical(c, shift)
    cum_m = jnp.where(same & (r <= c), 1.0, 0.0).astype(BF16)
    tot_m = jnp.where(same, 1.0, 0.0).astype(BF16)
    parts = jnp.concatenate(_split3(g), axis=0).astype(BF16)
    cum3 = jnp.dot(parts, cum_m, preferred_element_type=F32)
    tot3 = jnp.dot(parts, tot_m, preferred_element_type=F32)
    gc = cum3[0:h] + cum3[h:2 * h] + cum3[2 * h:3 * h]
    gl = tot3[0:h] + tot3[h:2 * h] + tot3[2 * h:3 * h]
    o_ref[0:h, :] = beta
    o_ref[h:2 * h, :] = gc
    o_ref[2 * h:3 * h, :] = jnp.exp(gl - gc)
    o_ref[3 * h:4 * h, :] = jnp.exp(gl)


def _gates(p_small, a_log, dt_bias):
    n = p_small.shape[0]
    tm = 512
    h = GDN_HEADS
    return pl.pallas_call(
        _gate_body,
        out_shape=jax.ShapeDtypeStruct((4 * h, n), F32),
        grid=(n // tm,),
        in_specs=[pl.BlockSpec((tm, LANES), lambda i: (i, 1)),
                  pl.BlockSpec((h, 1), lambda i: (0, 0)),
                  pl.BlockSpec((h, 1), lambda i: (0, 0))],
        out_specs=pl.BlockSpec((4 * h, tm), lambda i: (0, i)),
        compiler_params=_params("arbitrary"),
        name="gdn_gates",
    )(p_small, a_log.reshape(h, 1), dt_bias.reshape(h, 1))


def _intra_body(q_ref, k_ref, v_ref, kt_ref, beta_ref, gc_ref, dec_ref,
                u_ref, w_ref, qe_ref, kdt_ref, attn_ref, *, nc):
    hh = pl.program_id(1)
    ri = lax.broadcasted_iota(jnp.int32, (CHUNK, CHUNK), 0)
    ci = lax.broadcasted_iota(jnp.int32, (CHUNK, CHUNK), 1)
    eye = ri == ci
    lower = ri >= ci
    strict = ri > ci
    kt = kt_ref[...]
    kdt_ref[...] = (kt.astype(F32) * dec_ref[pl.ds(hh, 1), :]).astype(BF16)

    def to_col(rowvec):
        return jnp.sum(jnp.where(eye, jnp.broadcast_to(rowvec, (CHUNK, CHUNK)), 0.0), axis=1, keepdims=True)

    cs = range(nc)
    rows = [slice(c * CHUNK, (c + 1) * CHUNK) for c in cs]
    g_row = [gc_ref[c:c + 1, :] for c in cs]
    g_col = [to_col(g) for g in g_row]
    b_col = [to_col(beta_ref[c:c + 1, :]) for c in cs]
    eg_col = [jnp.exp(g) for g in g_col]
    decay = [jnp.where(lower, jnp.exp(jnp.where(lower, gc - gr, 0.0)), 0.0) for gc, gr in zip(g_col, g_row)]
    ktc = [kt[:, r] for r in rows]
    qb = [q_ref[r, :] for r in rows]
    kb = [k_ref[r, :].astype(F32) * b for r, b in zip(rows, b_col)]
    rhs = [jnp.concatenate([v_ref[r, :].astype(F32) * b, k * e], axis=1)
           for r, b, k, e in zip(rows, b_col, kb, eg_col)]
    kk = [_bdot(k, t) for k, t in zip(kb, ktc)]
    qk = [_bdot(q, t) for q, t in zip(qb, ktc)]
    lmat = [jnp.where(strict, k * d, 0.0) for k, d in zip(kk, decay)]
    xt = None
    s = 1
    while s < CHUNK:
        shift = s.bit_length() - 1
        bi = lax.shift_right_logical(ri, shift)
        bj = lax.shift_right_logical(ci, shift)
        pair = (lax.shift_right_logical(bi, 1) == lax.shift_right_logical(bj, 1)) & ((bi & 1) == 1) & ((bj & 1) == 0)
        cblk = [jnp.where(pair, m, 0.0) for m in lmat]
        if xt is None:
            xt = [-c for c in cblk]
        else:
            y = [c + _bdot(x, c) for x, c in zip(xt, cblk)]
            xt = [x - (a + _bdot(a, x)) for x, a in zip(xt, y)]
        s *= 2
    sol = [r + _bdot(x, r) for x, r in zip(xt, rhs)]
    for c in cs:
        u_ref[rows[c], :] = sol[c][:, 0:GDN_DV].astype(BF16)
        w_ref[rows[c], :] = sol[c][:, GDN_DV:GDN_DV + GDN_DK].astype(BF16)
        qe_ref[rows[c], :] = (qb[c].astype(F32) * eg_col[c]).astype(BF16)
        attn_ref[rows[c], :] = jnp.where(lower, qk[c] * decay[c], 0.0).astype(BF16)


def _intra(qn, kn, vn, kt, gate_rows, gate_flat, batch, seq):
    n = qn.shape[0]
    h = GDN_HEADS
    tt = min(2048, seq)
    nc = tt // CHUNK
    per_b = seq // tt
    wide = jax.ShapeDtypeStruct((n, h * LANES), BF16)
    row_spec = pl.BlockSpec((tt, LANES), lambda b, hh, t: (b * per_b + t, hh))
    return pl.pallas_call(
        functools.partial(_intra_body, nc=nc),
        out_shape=(wide, wide, wide,
                   jax.ShapeDtypeStruct((batch, h, LANES, seq), BF16),
                   jax.ShapeDtypeStruct((batch, h, seq, CHUNK), BF16)),
        grid=(batch, h, per_b),
        in_specs=[row_spec, row_spec, row_spec,
                  pl.BlockSpec((None, None, LANES, tt), lambda b, hh, t: (b, hh, 0, t)),
                  pl.BlockSpec((None, None, nc, CHUNK), lambda b, hh, t: (hh, b, t, 0)),
                  pl.BlockSpec((None, None, nc, CHUNK), lambda b, hh, t: (h + hh, b, t, 0)),
                  pl.BlockSpec((h, tt), lambda b, hh, t: (2, b * per_b + t))],
        out_specs=(row_spec, row_spec, row_spec,
                   pl.BlockSpec((None, None, LANES, tt), lambda b, hh, t: (b, hh, 0, t)),
                   pl.BlockSpec((None, None, tt, CHUNK), lambda b, hh, t: (b, hh, t, 0))),
        compiler_params=_params("arbitrary", "arbitrary", "arbitrary"),
        name="gdn_intra",
    )(qn, kn, vn, kt, gate_rows, gate_rows, gate_flat)


def _scan_body(u_ref, w_ref, qe_ref, kdt_ref, attn_ref, egl_ref, z_ref, gn_ref, o_ref, s_scr, *, nct):
    t = pl.program_id(1)

    @pl.when(t == 0)
    def _():
        s_scr[...] = jnp.zeros(s_scr.shape, F32)

    gn = gn_ref[...]
    hs = range(GDN_HEADS)
    cols = [slice(h * LANES, (h + 1) * LANES) for h in hs]
    state = [s_scr[h] for h in hs]
    for c in range(nct):
        rows = slice(c * CHUNK, (c + 1) * CHUNK)
        wq = [jnp.concatenate([w_ref[rows, cl], qe_ref[rows, cl]], axis=0) for cl in cols]
        r = [jnp.dot(a, s.astype(BF16), preferred_element_type=F32) for a, s in zip(wq, state)]
        v_new = [(u_ref[rows, cl].astype(F32) - x[0:CHUNK]).astype(BF16) for cl, x in zip(cols, r)]
        eg = [egl_ref[h, pl.ds(t * nct + c, 1), :][:, 0:1] for h in hs]
        state = [s * e + jnp.dot(kdt_ref[h, :, c * CHUNK:(c + 1) * CHUNK], v, preferred_element_type=F32)
                 for h, s, e, v in zip(hs, state, eg, v_new)]
        o = [x[CHUNK:2 * CHUNK] + jnp.dot(attn_ref[h, rows, :], v, preferred_element_type=F32)
             for h, x, v in zip(hs, r, v_new)]
        for h in hs:
            on = o[h] * lax.rsqrt(jnp.mean(o[h] * o[h], axis=-1, keepdims=True) + EPS) * gn
            o_ref[rows, cols[h]] = (on * _silu(z_ref[rows, cols[h]].astype(F32))).astype(BF16)
    for h in hs:
        s_scr[h] = state[h]


def _scan(u, w, qe, kdt, attn, egl, p_main, gdn_norm, batch, seq, z_blk):
    n = u.shape[0]
    h = GDN_HEADS
    tt = min(1024, seq)
    nct = tt // CHUNK
    per_b = seq // tt
    wide_spec = pl.BlockSpec((tt, h * LANES), lambda b, t: (b * per_b + t, 0))
    return pl.pallas_call(
        functools.partial(_scan_body, nct=nct),
        out_shape=jax.ShapeDtypeStruct((n, h * LANES), BF16),
        grid=(batch, per_b),
        in_specs=[wide_spec, wide_spec, wide_spec,
                  pl.BlockSpec((None, h, LANES, tt), lambda b, t: (b, 0, 0, t)),
                  pl.BlockSpec((None, h, tt, CHUNK), lambda b, t: (b, 0, t, 0)),
                  pl.BlockSpec((h, None, seq // CHUNK, CHUNK), lambda b, t: (3, b, 0, 0)),
                  pl.BlockSpec((tt, h * LANES), lambda b, t: (b * per_b + t, z_blk)),
                  pl.BlockSpec((1, LANES), lambda b, t: (0, 0))],
        out_specs=wide_spec,
        scratch_shapes=[pltpu.VMEM((h, GDN_DK, GDN_DV), F32)],
        compiler_params=_params("arbitrary", "arbitrary"),
        name="gdn_scan",
    )(u, w, qe, kdt, attn, egl, p_main, gdn_norm)


def _merge_body(a_ref, b_ref, wa_ref, wb_ref, ga_ref, gb_ref, o_ref):
    ya = jnp.dot(a_ref[...], wa_ref[...], preferred_element_type=F32)
    yb = jnp.dot(b_ref[...], wb_ref[...], preferred_element_type=F32)
    o = _sigmoid(ga_ref[...].astype(F32)) * ya + _sigmoid(gb_ref[...].astype(F32)) * yb
    o_ref[...] = o.astype(o_ref.dtype)


def _merge(o_a, o_b, w_a, w_b, layer, p_main, d):
    n, ka = o_a.shape
    kb = o_b.shape[1]
    tm, tn = 512, d
    nb = d // tn
    return pl.pallas_call(
        _merge_body,
        out_shape=jax.ShapeDtypeStruct((n, d), BF16),
        grid=(n // tm, nb),
        in_specs=[pl.BlockSpec((tm, ka), lambda i, j: (i, 0)),
                  pl.BlockSpec((tm, kb), lambda i, j: (i, 0)),
                  pl.BlockSpec((None, ka, tn), lambda i, j: (layer, 0, j)),
                  pl.BlockSpec((None, kb, tn), lambda i, j: (layer, 0, j)),
                  pl.BlockSpec((tm, tn), lambda i, j: (i, j)),
                  pl.BlockSpec((tm, tn), lambda i, j: (i, nb + j))],
        out_specs=pl.BlockSpec((tm, tn), lambda i, j: (i, j)),
        compiler_params=_params("arbitrary", "arbitrary"),
        name="branch_merge",
    )(o_a, o_b, w_a, w_b, p_main, p_main)


def _resid_body(a_ref, w_ref, x_ref, gt_ref, o_ref):
    y = jnp.dot(a_ref[...], w_ref[...], preferred_element_type=F32)
    o_ref[...] = x_ref[...] + gt_ref[...] * y


def _resid_matmul(a, w, layer, x, gt, seq, tm, tn):
    n, k = a.shape
    d = w.shape[-1]
    tm = min(tm, seq)
    per_b = seq // tm
    return pl.pallas_call(
        _resid_body,
        out_shape=jax.ShapeDtypeStruct((n, d), F32),
        grid=(n // tm, d // tn),
        in_specs=[pl.BlockSpec((tm, k), lambda i, j: (i, 0)),
                  pl.BlockSpec((None, k, tn), lambda i, j: (layer, 0, j)),
                  pl.BlockSpec((tm, tn), lambda i, j: (i, j)),
                  pl.BlockSpec((None, 1, tn), lambda i, j: (i // per_b, 0, j))],
        out_specs=pl.BlockSpec((tm, tn), lambda i, j: (i, j)),
        compiler_params=_params("arbitrary", "arbitrary"),
        name="resid_matmul",
    )(a, w, x, gt)


def _ffn_up_body(x_ref, nw_ref, sc_ref, sh_ref, wg_ref, wu_ref, o_ref, h_even, h_odd, *, n_tiles):
    def norm(fill):
        _norm_slice(x_ref, nw_ref, sc_ref, sh_ref, fill)

    def matmul(cur):
        hb = cur[...]
        gate = jnp.dot(hb, wg_ref[...], preferred_element_type=F32)
        up = jnp.dot(hb, wu_ref[...], preferred_element_type=F32)
        o_ref[...] = (_silu(gate) * up).astype(o_ref.dtype)

    _shifted_dispatch(n_tiles, h_even, h_odd, norm, matmul)


def _ffn_up(x, nw, sc, sh, w_gu, layer, seq):
    n, d = x.shape
    dff = w_gu.shape[-1] // 2
    tm, tn = min(1024, seq), 512
    n_tiles = n // tm
    nb = dff // tn
    in_specs, orow, col = _shifted_specs(tm, d, seq // tm, n_tiles)
    return pl.pallas_call(
        functools.partial(_ffn_up_body, n_tiles=n_tiles),
        out_shape=jax.ShapeDtypeStruct((n, dff), BF16),
        grid=(n_tiles + 1, nb),
        in_specs=in_specs + [pl.BlockSpec((None, d, tn), lambda i, j: (layer, 0, col(i, j))),
                             pl.BlockSpec((None, d, tn), lambda i, j: (layer, 0, nb + col(i, j)))],
        out_specs=pl.BlockSpec((tm, tn), lambda i, j: (orow(i), col(i, j))),
        scratch_shapes=[pltpu.VMEM((tm, d), BF16), pltpu.VMEM((tm, d), BF16)],
        compiler_params=_params("arbitrary", "arbitrary"),
        name="ffn_up",
    )(x, nw, sc, sh, w_gu, w_gu)


def _final_norm_body(x_ref, w_ref, o_ref):
    o_ref[...] = _rms(x_ref[...], w_ref[...])


def _final_norm(x, w):
    n, d = x.shape
    tm = 512
    return pl.pallas_call(
        _final_norm_body,
        out_shape=jax.ShapeDtypeStruct((n, d), F32),
        grid=(n // tm,),
        in_specs=[pl.BlockSpec((tm, d), lambda i: (i, 0)), pl.BlockSpec((1, d), lambda i: (0, 0))],
        out_specs=pl.BlockSpec((tm, d), lambda i: (i, 0)),
        compiler_params=_params("arbitrary"),
        name="final_norm",
    )(x, w)


def _layout_w_in(w_in, d):
    wt = jnp.swapaxes(w_in, 1, 2).astype(BF16)
    o_kpe = Q_LORA + KV_LORA
    o_qkvz = o_kpe + QK_ROPE
    o_ba = o_qkvz + 2 * GDN_QK + 2 * GDN_V
    o_gates = o_ba + 2 * GDN_HEADS
    regions = [(o_gates, 2 * d), (o_qkvz, o_ba - o_qkvz), (0, o_kpe)]
    half = QK_ROPE // 2
    zeros = lambda rows: jnp.zeros((wt.shape[0], rows, wt.shape[2]), wt.dtype)
    w_small = jnp.concatenate([wt[:, o_kpe:o_kpe + half], zeros(LANES // 2 - half),
                               wt[:, o_kpe + half:o_qkvz], zeros(LANES // 2 - half),
                               wt[:, o_ba:o_gates], zeros(LANES - 2 * GDN_HEADS)], axis=1)
    return wt, regions, w_small


def _layout_w_uq(w_uq):
    depth, k, _ = w_uq.shape
    wt = jnp.swapaxes(w_uq, 1, 2).reshape(depth, MLA_HEADS, QK_NOPE + QK_ROPE, k)
    half = QK_ROPE // 2
    z = jnp.zeros((depth, MLA_HEADS, LANES // 2 - half, k), wt.dtype)
    wt = jnp.concatenate([wt[:, :, :QK_NOPE + half], z, wt[:, :, QK_NOPE + half:], z], axis=2)
    return wt.reshape(depth, MLA_HEADS * MLA_QK_PAD, k).astype(BF16)


def _rope_tables(positions):
    inv_freq = 1.0 / (ROPE_THETA ** (jnp.arange(0, QK_ROPE, 2, dtype=F32) / QK_ROPE))
    ang = positions.astype(F32).reshape(-1)[:, None] * inv_freq
    cos, sin = jnp.cos(ang), jnp.sin(ang)
    z = jnp.zeros_like(cos)
    return jnp.concatenate([cos, z, cos, z], axis=1), jnp.concatenate([-sin, z, sin, z], axis=1)


def kernel(x, c, positions, w_ada, b_ada, norm_mix, norm_ffn, w_in, q_a_norm, kv_a_norm, w_uq, w_ukv, w_o_mla,
           conv_w, A_log, dt_bias, gdn_norm, w_o_gdn, w_o, w_gate_up, w_down, final_norm):
    batch, seq, d = x.shape
    depth = w_ada.shape[0]
    n = batch * seq
    h = GDN_HEADS
    cos_t, sin_t = _rope_tables(positions)
    mod = _ada(c, w_ada, b_ada).reshape(depth, batch, 6, 1, d)
    xs = x.reshape(n, d)

    qkv_blk0 = 2 * d // LANES
    z_blk = (2 * d + 2 * GDN_QK + GDN_V) // (h * LANES)
    cq_blk = (2 * d + 2 * GDN_QK + 2 * GDN_V) // Q_LORA
    ckv_blk = cq_blk + 1

    w_in_t, in_regions, w_small = _layout_w_in(w_in, d)
    w_uq_b = _layout_w_uq(w_uq)
    w_ukv_b = w_ukv.astype(BF16)
    w_o_mla_b = w_o_mla.astype(BF16)
    w_o_gdn_b = w_o_gdn.astype(BF16)
    w_o_b = w_o.astype(BF16)
    w_gate_up_b = w_gate_up.astype(BF16)
    w_down_b = w_down.astype(BF16)

    for l in range(depth):
        sh_a, sc_a, gt_a, sh_f, sc_f, gt_f = [mod[l, :, i] for i in range(6)]
        p_main, p_small = _inproj(xs, norm_mix[l].reshape(1, d), sc_a, sh_a, w_in_t, in_regions, w_small, l, seq)

        q, kv, kpe = _mla_proj(p_main, p_small, cos_t, sin_t, q_a_norm[l].reshape(1, -1),
                               kv_a_norm[l].reshape(1, -1), w_uq_b, w_ukv_b, l, cq_blk, ckv_blk)
        o_a = _flash(q.reshape(batch, seq, -1), kv.reshape(batch, seq, -1), kpe.reshape(batch, seq, -1))
        o_a = o_a.reshape(n, -1)

        qn, kn, vn, kt = _conv(p_main, conv_w[l], batch, seq, qkv_blk0)
        gate_flat = _gates(p_small, A_log[l], dt_bias[l])
        gate_rows = gate_flat.reshape(4 * h, batch, seq // CHUNK, CHUNK)
        u, w, qe, kdt, attn = _intra(qn, kn, vn, kt, gate_rows, gate_flat, batch, seq)
        o_b = _scan(u, w, qe, kdt, attn, gate_rows, p_main, gdn_norm[l].reshape(1, -1), batch, seq, z_blk)

        merged = _merge(o_a, o_b, w_o_mla_b, w_o_gdn_b, l, p_main, d)
        xs = _resid_matmul(merged, w_o_b, l, xs, gt_a, seq, 512, d)

        act = _ffn_up(xs, norm_ffn[l].reshape(1, d), sc_f, sh_f, w_gate_up_b, l, seq)
        xs = _resid_matmul(act, w_down_b, l, xs, gt_f, seq, 1024, 512)

    return _final_norm(xs, final_norm.reshape(1, d)).reshape(batch, seq, d)
```
